```python
import jax, jax.numpy as jnp
from jax import lax
import numpy as np

D_MODEL = 1024
BATCH = 16
SEQ = 4096
DEPTH = 4

N_A_LAYERS = DEPTH // 2
CONV_WIDTH = 3
PATTERNS = ((128, 1), (512, 4), (2048, 16))
N_GROUPS = len(PATTERNS)
H_G = 8
HEAD_DIM = 64
D_FF = 4 * D_MODEL
EPS = 1e-5
ALIBI_MAX_BIAS = 8.0
NEG_INF = -1e30

kernel_name = 'yoco_shortconv_dilated_attn_trunk'


def rmsnorm(x, g):
    xf = x.astype(jnp.float32)
    y = xf * lax.rsqrt(jnp.mean(xf * xf, axis=-1, keepdims=True) + EPS)
    return (y * g.astype(jnp.float32)).astype(x.dtype)


def short_conv_mixer(h, w_in, conv_w, w_out):
    S = h.shape[1]
    b, c, u = jnp.split(h @ w_in, 3, axis=-1)
    up = jnp.pad(c * u, ((0, 0), (CONV_WIDTH - 1, 0), (0, 0)))
    conv = sum(conv_w[k] * up[:, CONV_WIDTH - 1 - k: CONV_WIDTH - 1 - k + S] for k in range(CONV_WIDTH))
    return (b * conv) @ w_out


def padded_len(S, window):
    return -(-S // window) * window


def to_blocks(t, dilation, blk, seq_pad):
    B, S = t.shape[:2]
    t = jnp.pad(t, [(0, 0), (0, seq_pad - S)] + [(0, 0)] * (t.ndim - 2))
    return t.reshape(B, seq_pad // (dilation * blk), blk, dilation, *t.shape[2:])


def kv_context(tb):
    prev = jnp.pad(tb, [(0, 0), (1, 0)] + [(0, 0)] * 4)[:, :-1]
    return jnp.concatenate([prev, tb], axis=2)


def dilated_branch(q, k_ctx, v_ctx, window, dilation, slopes):
    B, S, H, dh = q.shape
    blk = window // dilation
    seq_pad = padded_len(S, window)
    qb = to_blocks(q, dilation, blk, seq_pad)
    nb = qb.shape[1]
    s = jnp.einsum('bnqrhd,bnkrhd->bnrhqk', qb, k_ctx).astype(jnp.float32) * (dh ** -0.5)
    a = jnp.arange(blk)[:, None]
    c = jnp.arange(2 * blk)[None, :]
    j = blk + a - c
    n = jnp.arange(nb)[:, None, None]
    valid = (j >= 0) & (j <= blk) & ((n > 0) | (c >= blk))
    bias = -slopes[:, None, None] * (dilation * j).astype(jnp.float32)
    s = jnp.where(valid[None, :, None, None], s + bias, NEG_INF)
    lse = jax.nn.logsumexp(s, axis=-1)
    p = jnp.exp(s - lse[..., None]).astype(v_ctx.dtype)
    o = jnp.einsum('bnrhqk,bnkrhd->bnqrhd', p, v_ctx).reshape(B, seq_pad, H, dh)[:, :S]
    lse = jnp.transpose(lse, (0, 1, 4, 2, 3)).reshape(B, seq_pad, H)[:, :S]
    return o, lse


def setup_inputs(seed: int = 0) -> dict:
    key = jax.random.key(seed)
    ks = jax.random.split(key, 14)
    n_a = N_A_LAYERS
    n_b = DEPTH - N_A_LAYERS
    qw = N_GROUPS * H_G * HEAD_DIM
    f32 = jnp.float32
    nrm = lambda k, shape: jax.random.normal(k, shape, f32)
    return {
        'x': nrm(ks[0], (BATCH, SEQ, D_MODEL)),
        'norm_mix': 1.0 + 0.05 * nrm(ks[1], (DEPTH, D_MODEL)),
        'norm_mlp': 1.0 + 0.05 * nrm(ks[2], (DEPTH, D_MODEL)),
        'w_a_in': nrm(ks[3], (n_a, D_MODEL, 3 * D_MODEL)) * D_MODEL ** -0.5,
        'conv_w': nrm(ks[4], (n_a, CONV_WIDTH, D_MODEL)) * CONV_WIDTH ** -0.5,
        'w_a_out': nrm(ks[5], (n_a, D_MODEL, D_MODEL)) * D_MODEL ** -0.5,
        'norm_kv': 1.0 + 0.05 * nrm(ks[6], (D_MODEL,)),
        'w_kv': nrm(ks[7], (D_MODEL, 2 * qw)) * D_MODEL ** -0.5,
        'w_q': nrm(ks[8], (n_b, D_MODEL, qw)) * D_MODEL ** -0.5,
        'w_o': nrm(ks[9], (n_b, H_G * HEAD_DIM, D_MODEL)) * (H_G * HEAD_DIM) ** -0.5,
        'w_up': nrm(ks[10], (DEPTH, D_MODEL, D_FF)) * D_MODEL ** -0.5,
        'w_down': nrm(ks[11], (DEPTH, D_FF, D_MODEL)) * (0.5 * D_FF ** -0.5),
        'norm_final': 1.0 + 0.05 * nrm(ks[12], (D_MODEL,)),
    }


def reference(x, norm_mix, norm_mlp, w_a_in, conv_w, w_a_out, norm_kv, w_kv, w_q, w_o, w_up, w_down, norm_final):
    B, S, _ = x.shape
    slopes = 2.0 ** (-ALIBI_MAX_BIAS * jnp.arange(1, H_G + 1, dtype=jnp.float32) / H_G)
    h = x
    shared = []
    for l in range(DEPTH):
        if l < N_A_LAYERS:
            h = h + short_conv_mixer(rmsnorm(h, norm_mix[l]), w_a_in[l], conv_w[l], w_a_out[l])
        else:
            if l == N_A_LAYERS:
                kv = (rmsnorm(h, norm_kv) @ w_kv).reshape(B, S, N_GROUPS, 2, H_G, HEAD_DIM)
                for g, (window, dil) in enumerate(PATTERNS):
                    blk = window // dil
                    sp = padded_len(S, window)
                    shared.append((kv_context(to_blocks(kv[:, :, g, 0], dil, blk, sp)),
                                   kv_context(to_blocks(kv[:, :, g, 1], dil, blk, sp))))
            i = l - N_A_LAYERS
            q = (rmsnorm(h, norm_mix[l]) @ w_q[i]).reshape(B, S, N_GROUPS, H_G, HEAD_DIM)
            outs, lses = [], []
            for g, (window, dil) in enumerate(PATTERNS):
                o_g, lse_g = dilated_branch(q[:, :, g], shared[g][0], shared[g][1], window, dil, slopes)
                outs.append(o_g.astype(jnp.float32))
                lses.append(lse_g)
            wts = jax.nn.softmax(jnp.stack(lses), axis=0)
            o = jnp.sum(wts[..., None] * jnp.stack(outs), axis=0).astype(h.dtype)
            h = h + o.reshape(B, S, H_G * HEAD_DIM) @ w_o[i]
        hn = rmsnorm(h, norm_mlp[l])
        h = h + jnp.square(jax.nn.relu(hn @ w_up[l])) @ w_down[l]
    return rmsnorm(h, norm_final)
```

```python
import functools

import jax
import jax.numpy as jnp
from jax import lax
from jax.experimental import pallas as pl
from jax.experimental.pallas import tpu as pltpu

PATTERNS = ((128, 1), (512, 4), (2048, 16))
N_GROUPS = len(PATTERNS)
H_G = 8
HEAD_DIM = 64
GROUP_WIDTH = H_G * HEAD_DIM
EPS = 1e-5
ALIBI_MAX_BIAS = 8.0
NEG_INF = -1e30
CONV_WIDTH = 3

LANES = 128
SUBLANES = 8
KEY_BLOCK = 128
HEADS_PER_LANE_TILE = LANES // HEAD_DIM
LSE_LANES_PER_HEAD = LANES // H_G

VMEM_LIMIT_BYTES = 56 * 1024 * 1024

F32 = jnp.float32
BF16 = jnp.bfloat16


def _params(n_axes):
    return pltpu.CompilerParams(
        dimension_semantics=("arbitrary",) * n_axes,
        vmem_limit_bytes=VMEM_LIMIT_BYTES,
    )


def _resident(shape):
    zeros = (0,) * len(shape)
    return pl.BlockSpec(shape, lambda *_: zeros, pipeline_mode=pl.Buffered(1))


def _rmsnorm(x, g):
    ms = jnp.mean(x * x, axis=-1, keepdims=True)
    return x * lax.rsqrt(ms + EPS) * g


def _conv_mixer_kernel(h_ref, g_ref, win_ref, cw_ref, wout_ref, o_ref, cu_ref, *, tiles_per_seq):
    tm, d = h_ref.shape
    x = h_ref[...]
    hn = _rmsnorm(x, g_ref[...]).astype(BF16)
    bcu = jnp.dot(hn, win_ref[...], preferred_element_type=F32)
    cu = bcu[:, d:2 * d] * bcu[:, 2 * d:]

    @pl.when(pl.program_id(0) % tiles_per_seq == 0)
    def _():
        cu_ref[0:SUBLANES, :] = jnp.zeros((SUBLANES, d), F32)

    cu_ref[SUBLANES:SUBLANES + tm, :] = cu
    cw = cw_ref[...]
    conv = cw[0:1, :] * cu
    for k in range(1, CONV_WIDTH):
        conv = conv + cw[k:k + 1, :] * cu_ref[SUBLANES - k:SUBLANES - k + tm, :]
    y = jnp.dot((bcu[:, :d] * conv).astype(BF16), wout_ref[...], preferred_element_type=F32)
    o_ref[...] = x + y
    cu_ref[0:SUBLANES, :] = cu_ref[tm:tm + SUBLANES, :]


def _conv_mixer(h, g, w_in, conv_w, w_out, *, seq, tm):
    t, d = h.shape
    return pl.pallas_call(
        functools.partial(_conv_mixer_kernel, tiles_per_seq=seq // tm),
        grid=(t // tm,),
        in_specs=[
            pl.BlockSpec((tm, d), lambda i: (i, 0)),
            _resident((1, d)),
            _resident((d, 3 * d)),
            _resident((CONV_WIDTH, d)),
            _resident((d, d)),
        ],
        out_specs=pl.BlockSpec((tm, d), lambda i: (i, 0)),
        out_shape=jax.ShapeDtypeStruct((t, d), F32),
        scratch_shapes=[pltpu.VMEM((tm + SUBLANES, d), F32)],
        compiler_params=_params(1),
        name="conv_mixer",
    )(h, g, w_in, conv_w, w_out)


def _mlp_body(x, g_ref, wup_ref, wdn_ref):
    hn = _rmsnorm(x, g_ref[...]).astype(BF16)
    a = jnp.dot(hn, wup_ref[...], preferred_element_type=F32)
    a = jnp.maximum(a, 0.0)
    a = (a * a).astype(BF16)
    return x + jnp.dot(a, wdn_ref[...], preferred_element_type=F32)


def _mlp_kernel(h_ref, g_ref, wup_ref, wdn_ref, *rest, final_norm):
    if final_norm:
        gf_ref, o_ref = rest
    else:
        (o_ref,) = rest
    y = _mlp_body(h_ref[...], g_ref, wup_ref, wdn_ref)
    if final_norm:
        y = _rmsnorm(y, gf_ref[...])
    o_ref[...] = y


def _attn_mlp_kernel(h_ref, o0_ref, o1_ref, o2_ref, l0_ref, l1_ref, l2_ref, expand_ref, wo_ref,
                     g_ref, wup_ref, wdn_ref, *rest, final_norm):
    if final_norm:
        gf_ref, o_ref = rest
    else:
        (o_ref,) = rest
    lses = [l0_ref[...], l1_ref[...], l2_ref[...]]
    m = jnp.maximum(jnp.maximum(lses[0], lses[1]), lses[2])
    es = [jnp.exp(l - m) for l in lses]
    den = es[0] + es[1] + es[2]
    comb = None
    for e, og_ref in zip(es, (o0_ref, o1_ref, o2_ref)):
        w = e / den
        w_hi = w.astype(BF16)
        w_lo = (w - w_hi.astype(F32)).astype(BF16)
        w_full = jnp.dot(jnp.concatenate([w_hi, w_lo], axis=1), expand_ref[...],
                         preferred_element_type=F32)
        term = w_full * og_ref[...].astype(F32)
        comb = term if comb is None else comb + term
    x = h_ref[...] + jnp.dot(comb.astype(BF16), wo_ref[...], preferred_element_type=F32)
    y = _mlp_body(x, g_ref, wup_ref, wdn_ref)
    if final_norm:
        y = _rmsnorm(y, gf_ref[...])
    o_ref[...] = y


def _mlp(h, g, w_up, w_down, g_final, *, tm, attn=None):
    t, d = h.shape
    d_ff = w_up.shape[1]
    row = lambda width: pl.BlockSpec((tm, width), lambda i: (i, 0))
    args = [h]
    in_specs = [row(d)]
    if attn is not None:
        outs, lses, expand, w_o = attn
        args += [*outs, *lses, expand, w_o]
        in_specs += [row(GROUP_WIDTH)] * N_GROUPS + [row(LANES)] * N_GROUPS
        in_specs += [_resident(expand.shape), _resident(w_o.shape)]
        body = _attn_mlp_kernel
    else:
        body = _mlp_kernel
    args += [g, w_up, w_down]
    in_specs += [_resident((1, d)), _resident((d, d_ff)), _resident((d_ff, d))]
    if g_final is not None:
        args.append(g_final)
        in_specs.append(_resident((1, d)))
    return pl.pallas_call(
        functools.partial(body, final_norm=g_final is not None),
        grid=(t // tm,),
        in_specs=in_specs,
        out_specs=row(d),
        out_shape=jax.ShapeDtypeStruct((t, d), F32),
        compiler_params=_params(1),
        name="attn_mlp" if attn is not None else "mlp",
    )(*args)


def _proj_kernel(*refs, n_proj, scales):
    h_ref = refs[0]
    x = h_ref[...]
    inv = lax.rsqrt(jnp.mean(x * x, axis=-1, keepdims=True) + EPS)
    xn = x * inv
    for p in range(n_proj):
        g_ref, w_ref = refs[1 + 2 * p], refs[2 + 2 * p]
        o_ref = refs[1 + 2 * n_proj + p]
        y = jnp.dot((xn * g_ref[...]).astype(BF16), w_ref[...], preferred_element_type=F32)
        if scales[p] != 1.0:
            y = y * scales[p]
        o_ref[...] = y.astype(o_ref.dtype)


def _proj(h, projs, *, tm):
    t, d = h.shape
    args = [h]
    in_specs = [pl.BlockSpec((tm, d), lambda i: (i, 0))]
    out_specs, out_shapes = [], []
    for g, w, _ in projs:
        args += [g, w]
        in_specs += [_resident((1, d)), _resident(w.shape)]
        out_specs.append(pl.BlockSpec((tm, w.shape[1]), lambda i: (i, 0)))
        out_shapes.append(jax.ShapeDtypeStruct((t, w.shape[1]), BF16))
    return pl.pallas_call(
        functools.partial(_proj_kernel, n_proj=len(projs), scales=tuple(s for _, _, s in projs)),
        grid=(t // tm,),
        in_specs=in_specs,
        out_specs=out_specs,
        out_shape=out_shapes,
        compiler_params=_params(1),
        name="norm_proj",
    )(*args)


def _attn_kernel(q_ref, kc_ref, vc_ref, kp_ref, vp_ref, bprev_ref, bcur_ref, o_ref, lse_ref):
    qb = q_ref.shape[0]
    first_block = pl.program_id(2) == 0
    lane = lax.broadcasted_iota(jnp.int32, (KEY_BLOCK, LANES), 1)
    nt = (((1,), (1,)), ((), ()))
    for j in range(qb // KEY_BLOCK):
        rows = slice(j * KEY_BLOCK, (j + 1) * KEY_BLOCK)
        prev_rows = slice((j - 1) * KEY_BLOCK, j * KEY_BLOCK)
        lse_tile = jnp.zeros((KEY_BLOCK, LANES), F32)
        for hp in range(H_G // HEADS_PER_LANE_TILE):
            cols = slice(hp * LANES, (hp + 1) * LANES)
            q2 = q_ref[rows, cols]
            k_cur, v_cur = kc_ref[rows, cols], vc_ref[rows, cols]
            if j == 0:
                k_prev, v_prev = kp_ref[:, cols], vp_ref[:, cols]
            else:
                k_prev, v_prev = kc_ref[prev_rows, cols], vc_ref[prev_rows, cols]
            o2 = None
            for e in range(HEADS_PER_LANE_TILE):
                h = hp * HEADS_PER_LANE_TILE + e
                in_head = (lane // HEAD_DIM) == e
                q1 = jnp.where(in_head, q2, jnp.zeros_like(q2))
                if j == 0:
                    b_prev = bprev_ref[jnp.where(first_block, 1, 0), h]
                else:
                    b_prev = bprev_ref[0, h]
                s_prev = lax.dot_general(q1, k_prev, nt, preferred_element_type=F32) + b_prev
                s_cur = lax.dot_general(q1, k_cur, nt, preferred_element_type=F32) + bcur_ref[h]
                m = jnp.maximum(jnp.max(s_prev, axis=1, keepdims=True),
                                jnp.max(s_cur, axis=1, keepdims=True))
                p_prev = jnp.exp(s_prev - m)
                p_cur = jnp.exp(s_cur - m)
                l = jnp.sum(p_prev, axis=1, keepdims=True) + jnp.sum(p_cur, axis=1, keepdims=True)
                pv = (jnp.dot(p_prev.astype(BF16), v_prev, preferred_element_type=F32)
                      + jnp.dot(p_cur.astype(BF16), v_cur, preferred_element_type=F32))
                o1 = pv / l
                o2 = o1 if o2 is None else jnp.where(in_head, o1, o2)
                lse = m + jnp.log(l)
                lse_tile = jnp.where((lane // LSE_LANES_PER_HEAD) == h, lse, lse_tile)
            o_ref[rows, cols] = o2.astype(o_ref.dtype)
        lse_ref[rows, :] = lse_tile


def _attention_group(q, kv, bias_prev, bias_cur, *, group, batch, seq, qb):
    dil = PATTERNS[group][1]
    sub = seq // dil
    qb = min(qb, sub)
    qw, kvw = q.shape[1], kv.shape[1]
    q_view = q.reshape(batch, sub, dil * qw)
    kv_view = kv.reshape(batch, sub, dil * kvw)
    q_per = qw // GROUP_WIDTH
    kv_per = kvw // GROUP_WIDTH
    blocks_per_step = qb // KEY_BLOCK

    blk = lambda rows, fn: pl.BlockSpec((None, rows, GROUP_WIDTH), fn)
    in_specs = [
        blk(qb, lambda b, r, n: (b, n, r * q_per + group)),
        blk(qb, lambda b, r, n: (b, n, r * kv_per + 2 * group)),
        blk(qb, lambda b, r, n: (b, n, r * kv_per + 2 * group + 1)),
        blk(KEY_BLOCK, lambda b, r, n: (b, jnp.maximum(n * blocks_per_step - 1, 0),
                                        r * kv_per + 2 * group)),
        blk(KEY_BLOCK, lambda b, r, n: (b, jnp.maximum(n * blocks_per_step - 1, 0),
                                        r * kv_per + 2 * group + 1)),
        _resident(bias_prev.shape),
        _resident(bias_cur.shape),
    ]
    out_specs = [
        blk(qb, lambda b, r, n: (b, n, r)),
        pl.BlockSpec((None, qb, LANES), lambda b, r, n: (b, n, r)),
    ]
    o, lse = pl.pallas_call(
        _attn_kernel,
        grid=(batch, dil, sub // qb),
        in_specs=in_specs,
        out_specs=out_specs,
        out_shape=[jax.ShapeDtypeStruct((batch, sub, dil * GROUP_WIDTH), BF16),
                   jax.ShapeDtypeStruct((batch, sub, dil * LANES), F32)],
        compiler_params=_params(3),
        name=f"dilated_attn_g{group}",
    )(q_view, kv_view, kv_view, kv_view, kv_view, bias_prev, bias_cur)
    return o.reshape(batch * seq, GROUP_WIDTH), lse.reshape(batch * seq, LANES)


def _alibi_tables(dilation):
    slopes = 2.0 ** (-ALIBI_MAX_BIAS * jnp.arange(1, H_G + 1, dtype=F32) / H_G)
    a = jnp.arange(KEY_BLOCK)[:, None]
    c = jnp.arange(2 * KEY_BLOCK)[None, :]
    j = KEY_BLOCK + a - c
    valid = (j >= 0) & (j <= KEY_BLOCK)
    bias = -slopes[:, None, None] * (dilation * j).astype(F32)
    bias = jnp.where(valid[None], bias, NEG_INF)
    prev, cur = bias[:, :, :KEY_BLOCK], bias[:, :, KEY_BLOCK:]
    return jnp.stack([prev, jnp.full_like(prev, NEG_INF)]), cur


def _head_expand_matrix():
    src = jnp.arange(LANES)[:, None]
    dst = jnp.arange(GROUP_WIDTH)[None, :]
    e = (src == (dst // HEAD_DIM) * LSE_LANES_PER_HEAD).astype(BF16)
    return jnp.concatenate([e, e], axis=0)


def kernel(x, norm_mix, norm_mlp, w_a_in, conv_w, w_a_out, norm_kv, w_kv, w_q, w_o, w_up, w_down,
           norm_final):
    batch, seq, d = x.shape
    depth = norm_mix.shape[0]
    n_a = w_a_in.shape[0]
    assert seq % max(w for w, _ in PATTERNS) == 0 and d % LANES == 0
    tm = 512
    qb = 256

    bf = lambda w: w.astype(BF16)
    gain = lambda g: g.reshape(1, d).astype(F32)
    h = x.reshape(batch * seq, d)
    expand = _head_expand_matrix()
    tables = [_alibi_tables(dil) for _, dil in PATTERNS]
    q_scale = HEAD_DIM ** -0.5
    kv = None
    for l in range(depth):
        g_final = gain(norm_final) if l == depth - 1 else None
        if l < n_a:
            h = _conv_mixer(h, gain(norm_mix[l]), bf(w_a_in[l]), conv_w[l].astype(F32),
                            bf(w_a_out[l]), seq=seq, tm=tm)
            h = _mlp(h, gain(norm_mlp[l]), bf(w_up[l]), bf(w_down[l]), g_final, tm=tm)
        else:
            i = l - n_a
            projs = [(gain(norm_mix[l]), bf(w_q[i]), q_scale)]
            if kv is None:
                projs.append((gain(norm_kv), bf(w_kv), 1.0))
                q, kv = _proj(h, projs, tm=tm)
            else:
                (q,) = _proj(h, projs, tm=tm)
            outs, lses = [], []
            for g in range(N_GROUPS):
                o_g, lse_g = _attention_group(q, kv, *tables[g], group=g, batch=batch, seq=seq,
                                              qb=qb)
                outs.append(o_g)
                lses.append(lse_g)
            h = _mlp(h, gain(norm_mlp[l]), bf(w_up[l]), bf(w_down[l]), g_final, tm=tm,
                     attn=(outs, lses, expand, bf(w_o[i])))
    return h.reshape(batch, seq, d)
```

```python
import functools

import jax
import jax.numpy as jnp
from jax import lax
from jax.experimental import pallas as pl
from jax.experimental.pallas import tpu as pltpu

PATTERNS = ((128, 1), (512, 4), (2048, 16))
N_GROUPS = len(PATTERNS)
H_G = 8
HEAD_DIM = 64
GROUP_WIDTH = H_G * HEAD_DIM
EPS = 1e-5
ALIBI_MAX_BIAS = 8.0
NEG_INF = -1e30
CONV_WIDTH = 3

LANES = 128
SUBLANES = 8
KEY_BLOCK = 128
HEADS_PER_TILE = LANES // HEAD_DIM
HEAD_PAIRS = H_G // HEADS_PER_TILE
LSE_LANES_PER_HEAD = LANES // H_G

MAX_DIL = PATTERNS[2][1]
SUPER = PATTERNS[2][0]
MID_DIL = PATTERNS[1][1]
TM = 512
CHUNK = TM // MAX_DIL
TILES_PER_SUPER = SUPER // TM
SUBBLOCKS = 4

VMEM_LIMIT_BYTES = 56 * 1024 * 1024

F32 = jnp.float32
BF16 = jnp.bfloat16


def _params(n_axes):
    return pltpu.CompilerParams(
        dimension_semantics=("arbitrary",) * n_axes,
        vmem_limit_bytes=VMEM_LIMIT_BYTES,
    )


def _resident(shape):
    zeros = (0,) * len(shape)
    return pl.BlockSpec(shape, lambda *_: zeros, pipeline_mode=pl.Buffered(1))


def _rmsnorm(x, g):
    ms = jnp.mean(x * x, axis=-1, keepdims=True)
    return x * lax.rsqrt(ms + EPS) * g


def _split3(x):
    hi = x.astype(BF16)
    r = x - hi.astype(F32)
    mid = r.astype(BF16)
    lo = (r - mid.astype(F32)).astype(BF16)
    return hi, mid, lo


def _conv_mixer_kernel(h_ref, g_ref, win_ref, cw_ref, wout_ref, o_ref, cu_ref, *, tiles_per_seq):
    tm, d = h_ref.shape
    x = h_ref[...]
    hn = _rmsnorm(x, g_ref[...]).astype(BF16)
    bcu = jnp.dot(hn, win_ref[...], preferred_element_type=F32)
    cu = bcu[:, d:2 * d] * bcu[:, 2 * d:]

    @pl.when(pl.program_id(0) % tiles_per_seq == 0)
    def _():
        cu_ref[0:SUBLANES, :] = jnp.zeros((SUBLANES, d), F32)

    cu_ref[SUBLANES:SUBLANES + tm, :] = cu
    cw = cw_ref[...]
    conv = cw[0:1, :] * cu
    for k in range(1, CONV_WIDTH):
        conv = conv + cw[k:k + 1, :] * cu_ref[SUBLANES - k:SUBLANES - k + tm, :]
    y = jnp.dot((bcu[:, :d] * conv).astype(BF16), wout_ref[...], preferred_element_type=F32)
    o_ref[...] = x + y
    cu_ref[0:SUBLANES, :] = cu_ref[tm:tm + SUBLANES, :]


def _conv_mixer(h, g, w_in, conv_w, w_out, *, seq):
    t, d = h.shape
    return pl.pallas_call(
        functools.partial(_conv_mixer_kernel, tiles_per_seq=seq // TM),
        grid=(t // TM,),
        in_specs=[
            pl.BlockSpec((TM, d), lambda i: (i, 0)),
            _resident((1, d)),
            _resident((d, 3 * d)),
            _resident((CONV_WIDTH, d)),
            _resident((d, d)),
        ],
        out_specs=pl.BlockSpec((TM, d), lambda i: (i, 0)),
        out_shape=jax.ShapeDtypeStruct((t, d), F32),
        scratch_shapes=[pltpu.VMEM((TM + SUBLANES, d), F32)],
        compiler_params=_params(1),
        name="conv_mixer",
    )(h, g, w_in, conv_w, w_out)


def _mlp_body(x, g_ref, wup_ref, wdn_ref):
    hn = _rmsnorm(x, g_ref[...]).astype(BF16)
    a = jnp.dot(hn, wup_ref[...], preferred_element_type=F32)
    a = jnp.maximum(a, 0.0)
    a = (a * a).astype(BF16)
    return x + jnp.dot(a, wdn_ref[...], preferred_element_type=F32)


def _mlp_kernel(h_ref, g_ref, wup_ref, wdn_ref, *rest, final_norm):
    if final_norm:
        gf_ref, o_ref = rest
    else:
        (o_ref,) = rest
    y = _mlp_body(h_ref[...], g_ref, wup_ref, wdn_ref)
    if final_norm:
        y = _rmsnorm(y, gf_ref[...])
    o_ref[...] = y


def _attn_mlp_kernel(h_ref, o0_ref, o1_ref, o2_ref, l0_ref, l1_ref, l2_ref, unperm_ref, expand_ref,
                     wo_ref, g_ref, wup_ref, wdn_ref, *rest, final_norm):
    if final_norm:
        gf_ref, o_ref = rest
    else:
        (o_ref,) = rest
    tm = h_ref.shape[0]
    unperm = unperm_ref[...]
    outs = [o0_ref[...]]
    lses = [l0_ref[...]]
    for og_ref, lg_ref in ((o1_ref, l1_ref), (o2_ref, l2_ref)):
        og = og_ref[...].reshape(tm, GROUP_WIDTH)
        outs.append(jnp.dot(unperm, og, preferred_element_type=F32).astype(BF16))
        parts = jnp.concatenate(_split3(lg_ref[...].reshape(tm, LANES)), axis=1)
        moved = jnp.dot(unperm, parts, preferred_element_type=F32)
        lses.append(moved[:, :LANES] + moved[:, LANES:2 * LANES] + moved[:, 2 * LANES:])
    m = jnp.maximum(jnp.maximum(lses[0], lses[1]), lses[2])
    es = [jnp.exp(l - m) for l in lses]
    den = es[0] + es[1] + es[2]
    comb = None
    for e, og in zip(es, outs):
        w = e / den
        w_hi = w.astype(BF16)
        w_lo = (w - w_hi.astype(F32)).astype(BF16)
        w_full = jnp.dot(jnp.concatenate([w_hi, w_lo], axis=1), expand_ref[...],
                         preferred_element_type=F32)
        term = w_full * og.astype(F32)
        comb = term if comb is None else comb + term
    x = h_ref[...] + jnp.dot(comb.astype(BF16), wo_ref[...], preferred_element_type=F32)
    y = _mlp_body(x, g_ref, wup_ref, wdn_ref)
    if final_norm:
        y = _rmsnorm(y, gf_ref[...])
    o_ref[...] = y


def _residue_major_spec(width, tiles_per_seq):
    def index(t):
        in_seq = t % tiles_per_seq
        return (t // tiles_per_seq, in_seq // TILES_PER_SUPER, 0, in_seq % TILES_PER_SUPER, 0)
    return pl.BlockSpec((None, None, MAX_DIL, CHUNK, width), index)


def _mlp(h, g, w_up, w_down, g_final, *, seq, attn=None):
    t, d = h.shape
    d_ff = w_up.shape[1]
    row = lambda width: pl.BlockSpec((TM, width), lambda i: (i, 0))
    args = [h]
    in_specs = [row(d)]
    if attn is not None:
        outs, lses, unperm, expand, w_o = attn
        args += [*outs, *lses, unperm, expand, w_o]
        rm = functools.partial(_residue_major_spec, tiles_per_seq=seq // TM)
        in_specs += [row(GROUP_WIDTH), rm(GROUP_WIDTH), rm(GROUP_WIDTH)]
        in_specs += [row(LANES), rm(LANES), rm(LANES)]
        in_specs += [_resident(unperm.shape), _resident(expand.shape), _resident(w_o.shape)]
        body = _attn_mlp_kernel
    else:
        body = _mlp_kernel
    args += [g, w_up, w_down]
    in_specs += [_resident((1, d)), _resident((d, d_ff)), _resident((d_ff, d))]
    if g_final is not None:
        args.append(g_final)
        in_specs.append(_resident((1, d)))
    return pl.pallas_call(
        functools.partial(body, final_norm=g_final is not None),
        grid=(t // TM,),
        in_specs=in_specs,
        out_specs=row(d),
        out_shape=jax.ShapeDtypeStruct((t, d), F32),
        compiler_params=_params(1),
        name="attn_mlp" if attn is not None else "mlp",
    )(*args)


def _proj_kernel(*refs, n_proj, scales):
    h_ref, perm_ref = refs[0], refs[1]
    x = h_ref[...]
    xn = x * lax.rsqrt(jnp.mean(x * x, axis=-1, keepdims=True) + EPS)
    for p in range(n_proj):
        g_ref, wnat_ref, wperm_ref = refs[2 + 3 * p: 5 + 3 * p]
        onat_ref, operm_ref = refs[2 + 3 * n_proj + 2 * p: 4 + 3 * n_proj + 2 * p]
        hn = (xn * g_ref[...]).astype(BF16)
        y = jnp.dot(hn, wnat_ref[...], preferred_element_type=F32)
        onat_ref[...] = (y * scales[p]).astype(onat_ref.dtype)
        hn_p = jnp.dot(perm_ref[...], hn, preferred_element_type=F32).astype(BF16)
        y = jnp.dot(hn_p, wperm_ref[...], preferred_element_type=F32)
        operm_ref[...] = (y * scales[p]).astype(operm_ref.dtype).reshape(operm_ref.shape)


def _proj(h, perm, projs, *, batch, seq):
    t, d = h.shape
    args = [h, perm]
    in_specs = [pl.BlockSpec((TM, d), lambda i: (i, 0)), _resident(perm.shape)]
    out_specs, out_shapes = [], []
    for g, w_nat, w_perm, _ in projs:
        args += [g, w_nat, w_perm]
        in_specs += [_resident((1, d)), _resident(w_nat.shape), _resident(w_perm.shape)]
        out_specs.append(pl.BlockSpec((TM, w_nat.shape[1]), lambda i: (i, 0)))
        out_shapes.append(jax.ShapeDtypeStruct((t, w_nat.shape[1]), BF16))
        out_specs.append(_residue_major_spec(w_perm.shape[1], seq // TM))
        out_shapes.append(jax.ShapeDtypeStruct(
            (batch, seq // SUPER, MAX_DIL, KEY_BLOCK, w_perm.shape[1]), BF16))
    return pl.pallas_call(
        functools.partial(_proj_kernel, n_proj=len(projs), scales=tuple(p[3] for p in projs)),
        grid=(t // TM,),
        in_specs=in_specs,
        out_specs=out_specs,
        out_shape=out_shapes,
        compiler_params=_params(1),
        name="norm_proj",
    )(*args)


def _attn_kernel(q_ref, kc_ref, vc_ref, kp_ref, vp_ref, bias_ref, o_ref, lse_ref, *, layout,
                 first_axis):
    first = jnp.where(pl.program_id(first_axis) == 0, 1, 0)
    lane = lax.broadcasted_iota(jnp.int32, (KEY_BLOCK, LANES), 1)
    nt = (((1,), (1,)), ((), ()))

    def block(ref, j, cols):
        if layout == "rows":
            return ref[j * KEY_BLOCK:(j + 1) * KEY_BLOCK, cols]
        if layout == "chunks":
            return ref[:, j, :, cols].reshape(KEY_BLOCK, LANES)
        return ref[j, :, cols]

    def prev_block(ref, cols):
        if layout == "chunks":
            return ref[:, :, cols].reshape(KEY_BLOCK, LANES)
        return ref[:, cols]

    def store(ref, j, cols, val):
        if layout == "rows":
            ref[j * KEY_BLOCK:(j + 1) * KEY_BLOCK, cols] = val
        elif layout == "chunks":
            ref[:, j, :, cols] = val.reshape(MID_DIL, KEY_BLOCK // MID_DIL, val.shape[-1])
        else:
            ref[j, :, cols] = val

    for j in range(SUBBLOCKS):
        lse_tile = jnp.zeros((KEY_BLOCK, LANES), F32)
        for hp in range(HEAD_PAIRS):
            cols = slice(hp * LANES, (hp + 1) * LANES)
            q2 = block(q_ref, j, cols)
            if layout == "blocks":
                k_prev, v_prev = kp_ref[j, :, cols], vp_ref[j, :, cols]
                variant = first
            elif j == 0:
                k_prev, v_prev = prev_block(kp_ref, cols), prev_block(vp_ref, cols)
                variant = first
            else:
                k_prev, v_prev = block(kc_ref, j - 1, cols), block(vc_ref, j - 1, cols)
                variant = 0
            k2 = jnp.concatenate([k_prev, block(kc_ref, j, cols)], axis=0)
            v2 = jnp.concatenate([v_prev, block(vc_ref, j, cols)], axis=0)
            zero = jnp.zeros_like(q2)
            qs = jnp.concatenate([jnp.where(lane < HEAD_DIM, q2, zero),
                                  jnp.where(lane >= HEAD_DIM, q2, zero)], axis=0)
            s = lax.dot_general(qs, k2, nt, preferred_element_type=F32) + bias_ref[variant, hp]
            m = jnp.max(s, axis=1, keepdims=True)
            p = jnp.exp(s - m)
            l = jnp.sum(p, axis=1, keepdims=True)
            pv = jnp.dot(p.astype(BF16), v2, preferred_element_type=F32)
            o = pv * (1.0 / l)
            store(o_ref, j, cols,
                  jnp.where(lane < HEAD_DIM, o[:KEY_BLOCK], o[KEY_BLOCK:]).astype(o_ref.dtype))
            lse = m + jnp.log(l)
            for e in range(HEADS_PER_TILE):
                h = hp * HEADS_PER_TILE + e
                lse_tile = jnp.where((lane // LSE_LANES_PER_HEAD) == h,
                                     lse[e * KEY_BLOCK:(e + 1) * KEY_BLOCK], lse_tile)
        store(lse_ref, j, slice(None), lse_tile)


def _attention_group(q, kv, bias, *, group, q_col, k_col, batch, seq):
    n_super = seq // SUPER
    rows = SUBBLOCKS * KEY_BLOCK
    if group == 0:
        layout, first_axis = "rows", 1
        grid = (batch, seq // rows)
        cur = lambda col: pl.BlockSpec((None, rows, GROUP_WIDTH), lambda b, n: (b, n, col))
        prev = lambda col: pl.BlockSpec(
            (None, KEY_BLOCK, GROUP_WIDTH),
            lambda b, n: (b, jnp.maximum(n * SUBBLOCKS - 1, 0), col))
        out_o = pl.BlockSpec((None, rows, GROUP_WIDTH), lambda b, n: (b, n, 0))
        out_l = pl.BlockSpec((None, rows, LANES), lambda b, n: (b, n, 0))
        o_shape, l_shape = (batch, seq, GROUP_WIDTH), (batch, seq, LANES)
        view = lambda a: a
    elif group == 1:
        layout, first_axis = "chunks", 1
        assert SUPER // PATTERNS[1][0] == SUBBLOCKS
        grid = (batch, n_super, MID_DIL)
        shape7 = lambda w: (batch, n_super, MID_DIL, MID_DIL, SUBBLOCKS, CHUNK, w)
        cur = lambda col: pl.BlockSpec((None, None, MID_DIL, None, SUBBLOCKS, CHUNK, GROUP_WIDTH),
                                       lambda b, n, r: (b, n, 0, r, 0, 0, col))
        prev = lambda col: pl.BlockSpec(
            (None, None, MID_DIL, None, None, CHUNK, GROUP_WIDTH),
            lambda b, n, r: (b, jnp.maximum(n - 1, 0), 0, r, SUBBLOCKS - 1, 0, col))
        out_o = cur(0)
        out_l = pl.BlockSpec((None, None, MID_DIL, None, SUBBLOCKS, CHUNK, LANES),
                             lambda b, n, r: (b, n, 0, r, 0, 0, 0))
        o_shape, l_shape = shape7(GROUP_WIDTH), shape7(LANES)
        view = lambda a: a.reshape(shape7(a.shape[-1]))
    else:
        layout, first_axis = "blocks", 1
        grid = (batch, n_super, MAX_DIL // SUBBLOCKS)
        cur = lambda col: pl.BlockSpec((None, None, SUBBLOCKS, KEY_BLOCK, GROUP_WIDTH),
                                       lambda b, n, r: (b, n, r, 0, col))
        prev = lambda col: pl.BlockSpec((None, None, SUBBLOCKS, KEY_BLOCK, GROUP_WIDTH),
                                        lambda b, n, r: (b, jnp.maximum(n - 1, 0), r, 0, col))
        out_o = cur(0)
        out_l = pl.BlockSpec((None, None, SUBBLOCKS, KEY_BLOCK, LANES),
                             lambda b, n, r: (b, n, r, 0, 0))
        o_shape = (batch, n_super, MAX_DIL, KEY_BLOCK, GROUP_WIDTH)
        l_shape = (batch, n_super, MAX_DIL, KEY_BLOCK, LANES)
        view = lambda a: a
    q, kv = view(q), view(kv)
    o, lse = pl.pallas_call(
        functools.partial(_attn_kernel, layout=layout, first_axis=first_axis),
        grid=grid,
        in_specs=[cur(q_col), cur(k_col), cur(k_col + 1), prev(k_col), prev(k_col + 1),
                  _resident(bias.shape)],
        out_specs=[out_o, out_l],
        out_shape=[jax.ShapeDtypeStruct(o_shape, BF16), jax.ShapeDtypeStruct(l_shape, F32)],
        compiler_params=_params(len(grid)),
        name=f"dilated_attn_g{group}",
    )(q, kv, kv, kv, kv, bias)
    if group == 0:
        return o.reshape(batch * seq, GROUP_WIDTH), lse.reshape(batch * seq, LANES)
    rm = lambda a: a.reshape(batch, n_super, MAX_DIL, KEY_BLOCK, a.shape[-1])
    return rm(o), rm(lse)


def _alibi_table(dilation, order):
    slopes = 2.0 ** (-ALIBI_MAX_BIAS * jnp.arange(1, H_G + 1, dtype=F32) / H_G)
    a = order[:, None]
    c = jnp.concatenate([order, KEY_BLOCK + order])[None, :]
    j = KEY_BLOCK + a - c
    valid = (j >= 0) & (j <= KEY_BLOCK)
    bias = -slopes[:, None, None] * (dilation * j).astype(F32)
    bias = jnp.where(valid[None], bias, NEG_INF)
    no_prev = jnp.where((c >= KEY_BLOCK)[None], bias, NEG_INF)
    both = jnp.stack([bias, no_prev])
    return both.reshape(2, HEAD_PAIRS, HEADS_PER_TILE * KEY_BLOCK, 2 * KEY_BLOCK)


def _head_expand_matrix():
    src = jnp.arange(LANES)[:, None]
    dst = jnp.arange(GROUP_WIDTH)[None, :]
    e = (src == (dst // HEAD_DIM) * LSE_LANES_PER_HEAD).astype(BF16)
    return jnp.concatenate([e, e], axis=0)


def _tile_permutation():
    dst = jnp.arange(TM)
    src = MAX_DIL * (dst % CHUNK) + dst // CHUNK
    return (jnp.arange(TM)[None, :] == src[:, None]).astype(BF16)


def kernel(x, norm_mix, norm_mlp, w_a_in, conv_w, w_a_out, norm_kv, w_kv, w_q, w_o, w_up, w_down,
           norm_final):
    batch, seq, d = x.shape
    depth = norm_mix.shape[0]
    n_a = w_a_in.shape[0]
    assert seq % SUPER == 0 and d % LANES == 0
    assert [w // dil for w, dil in PATTERNS] == [KEY_BLOCK] * N_GROUPS

    bf = lambda w: w.astype(BF16)
    gain = lambda g: g.reshape(1, d).astype(F32)
    h = x.reshape(batch * seq, d)
    expand = _head_expand_matrix()
    perm = _tile_permutation()
    unperm = perm.T
    idx = jnp.arange(KEY_BLOCK)
    orders = [idx, MID_DIL * (idx % CHUNK) + idx // CHUNK, idx]
    tables = [_alibi_table(dil, order) for (_, dil), order in zip(PATTERNS, orders)]
    q_scale = HEAD_DIM ** -0.5
    kv_nat = kv_rm = None
    for l in range(depth):
        g_final = gain(norm_final) if l == depth - 1 else None
        if l < n_a:
            h = _conv_mixer(h, gain(norm_mix[l]), bf(w_a_in[l]), conv_w[l].astype(F32),
                            bf(w_a_out[l]), seq=seq)
            h = _mlp(h, gain(norm_mlp[l]), bf(w_up[l]), bf(w_down[l]), g_final, seq=seq)
        else:
            i = l - n_a
            wq = bf(w_q[i])
            projs = [(gain(norm_mix[l]), wq[:, :GROUP_WIDTH], wq[:, GROUP_WIDTH:], q_scale)]
            if kv_nat is None:
                wkv = bf(w_kv)
                projs.append((gain(norm_kv), wkv[:, :2 * GROUP_WIDTH], wkv[:, 2 * GROUP_WIDTH:], 1.0))
                q_nat, q_rm, kv_nat, kv_rm = _proj(h, perm, projs, batch=batch, seq=seq)
            else:
                q_nat, q_rm = _proj(h, perm, projs, batch=batch, seq=seq)
            q_nat3 = q_nat.reshape(batch, seq, -1)
            kv_nat3 = kv_nat.reshape(batch, seq, -1)
            outs, lses = [], []
            for g in range(N_GROUPS):
                if g == 0:
                    o_g, lse_g = _attention_group(q_nat3, kv_nat3, tables[g], group=g, q_col=0,
                                                  k_col=0, batch=batch, seq=seq)
                else:
                    o_g, lse_g = _attention_group(q_rm, kv_rm, tables[g], group=g, q_col=g - 1,
                                                  k_col=2 * (g - 1), batch=batch, seq=seq)
                outs.append(o_g)
                lses.append(lse_g)
            h = _mlp(h, gain(norm_mlp[l]), bf(w_up[l]), bf(w_down[l]), g_final, seq=seq,
                     attn=(outs, lses, unperm, expand, bf(w_o[i])))
    return h.reshape(batch, seq, d)
```

```python
import functools

import jax
import jax.numpy as jnp
from jax import lax
from jax.experimental import pallas as pl
from jax.experimental.pallas import tpu as pltpu

PATTERNS = ((128, 1), (512, 4), (2048, 16))
N_GROUPS = len(PATTERNS)
H_G = 8
HEAD_DIM = 64
GROUP_WIDTH = H_G * HEAD_DIM
EPS = 1e-5
ALIBI_MAX_BIAS = 8.0
NEG_INF = -1e30
CONV_WIDTH = 3
LOG2E = 1.4426950408889634

LANES = 128
SUBLANES = 8
BF16_ROWS = 16
KEY_BLOCK = 128
HEADS_PER_TILE = LANES // HEAD_DIM
HEAD_PAIRS = H_G // HEADS_PER_TILE
STAT_LANES_PER_HEAD = LANES // H_G
STAT_HALF = STAT_LANES_PER_HEAD // 2

MAX_DIL = PATTERNS[2][1]
SUPER = PATTERNS[2][0]
MID_DIL = PATTERNS[1][1]
MID_CHUNK = KEY_BLOCK // MID_DIL
TM = 512
CHUNK = TM // MAX_DIL
TILES_PER_SUPER = SUPER // TM
PERM_ROWS = MAX_DIL * BF16_ROWS
PERM_PARTS = TM // PERM_ROWS
UNITS_PER_STEP = 16

VMEM_LIMIT_BYTES = 56 * 1024 * 1024

F32 = jnp.float32
BF16 = jnp.bfloat16


def _params(n_axes):
    return pltpu.CompilerParams(
        dimension_semantics=("arbitrary",) * n_axes,
        vmem_limit_bytes=VMEM_LIMIT_BYTES,
    )


def _resident(shape):
    zeros = (0,) * len(shape)
    return pl.BlockSpec(shape, lambda *_: zeros, pipeline_mode=pl.Buffered(1))


def _rmsnorm(x, g):
    ms = jnp.mean(x * x, axis=-1, keepdims=True)
    return x * lax.rsqrt(ms + EPS) * g


def _split3(x):
    hi = x.astype(BF16)
    r = x - hi.astype(F32)
    mid = r.astype(BF16)
    lo = (r - mid.astype(F32)).astype(BF16)
    return hi, mid, lo


def _to_residue_major(perm, x):
    parts = [jnp.dot(perm, x[s * PERM_ROWS:(s + 1) * PERM_ROWS], preferred_element_type=F32)
             .reshape(MAX_DIL, BF16_ROWS, x.shape[-1]) for s in range(PERM_PARTS)]
    return jnp.concatenate(parts, axis=1)


def _to_natural(unperm, x):
    parts = [jnp.dot(unperm, x[:, s * BF16_ROWS:(s + 1) * BF16_ROWS, :].reshape(PERM_ROWS, x.shape[-1]),
                     preferred_element_type=F32) for s in range(PERM_PARTS)]
    return jnp.concatenate(parts, axis=0)


def _conv_mixer_kernel(h_ref, g_ref, win_ref, cw_ref, wout_ref, o_ref, cu_ref, *, tiles_per_seq):
    tm, d = h_ref.shape
    x = h_ref[...]
    hn = _rmsnorm(x, g_ref[...]).astype(BF16)
    c_u = jnp.dot(hn, win_ref[:, d:], preferred_element_type=F32)
    cu = c_u[:, :d] * c_u[:, d:]

    @pl.when(pl.program_id(0) % tiles_per_seq == 0)
    def _():
        cu_ref[0:SUBLANES, :] = jnp.zeros((SUBLANES, d), F32)

    cu_ref[SUBLANES:SUBLANES + tm, :] = cu
    cw = cw_ref[...]
    conv = cw[0:1, :] * cu
    for k in range(1, CONV_WIDTH):
        conv = conv + cw[k:k + 1, :] * cu_ref[SUBLANES - k:SUBLANES - k + tm, :]
    b = jnp.dot(hn, win_ref[:, :d], preferred_element_type=F32)
    y = jnp.dot((b * conv).astype(BF16), wout_ref[...], preferred_element_type=F32)
    o_ref[...] = x + y
    cu_ref[0:SUBLANES, :] = cu_ref[tm:tm + SUBLANES, :]


def _conv_mixer(h, g, w_in, conv_w, w_out, *, seq):
    t, d = h.shape
    return pl.pallas_call(
        functools.partial(_conv_mixer_kernel, tiles_per_seq=seq // TM),
        grid=(t // TM,),
        in_specs=[
            pl.BlockSpec((TM, d), lambda i: (i, 0)),
            _resident((1, d)),
            _resident((d, 3 * d)),
            _resident((CONV_WIDTH, d)),
            _resident((d, d)),
        ],
        out_specs=pl.BlockSpec((TM, d), lambda i: (i, 0)),
        out_shape=jax.ShapeDtypeStruct((t, d), F32),
        scratch_shapes=[pltpu.VMEM((TM + SUBLANES, d), F32)],
        compiler_params=_params(1),
        name="conv_mixer",
    )(h, g, w_in, conv_w, w_out)


def _mlp_body(x, g_ref, wup_ref, wdn_ref):
    hn = _rmsnorm(x, g_ref[...]).astype(BF16)
    a = jnp.dot(hn, wup_ref[...], preferred_element_type=F32)
    a = jnp.maximum(a, 0.0)
    a = (a * a).astype(BF16)
    return x + jnp.dot(a, wdn_ref[...], preferred_element_type=F32)


def _mlp_kernel(h_ref, g_ref, wup_ref, wdn_ref, *rest, final_norm):
    if final_norm:
        gf_ref, o_ref = rest
    else:
        (o_ref,) = rest
    y = _mlp_body(h_ref[...], g_ref, wup_ref, wdn_ref)
    if final_norm:
        y = _rmsnorm(y, gf_ref[...])
    o_ref[...] = y


def _attn_mlp_kernel(h_ref, o0_ref, o1_ref, o2_ref, s0_ref, s1_ref, s2_ref, unperm_ref, expand_ref,
                     wo_ref, g_ref, wup_ref, wdn_ref, *rest, final_norm):
    if final_norm:
        gf_ref, o_ref = rest
    else:
        (o_ref,) = rest
    tm = h_ref.shape[0]
    unperm = unperm_ref[...]
    accs = [o0_ref[...]]
    stats = [s0_ref[...]]
    for og_ref, sg_ref in ((o1_ref, s1_ref), (o2_ref, s2_ref)):
        accs.append(_to_natural(unperm, og_ref[...]).astype(BF16))
        parts = jnp.concatenate(_split3(sg_ref[...]), axis=2)
        moved = _to_natural(unperm, parts)
        stats.append(moved[:, :LANES] + moved[:, LANES:2 * LANES] + moved[:, 2 * LANES:])
    lane = lax.broadcasted_iota(jnp.int32, (tm, LANES), 1)
    on_max_lanes = (lane % STAT_LANES_PER_HEAD) < STAT_HALF
    m_all = jnp.maximum(jnp.maximum(stats[0], stats[1]), stats[2])
    scale = [jnp.exp2(st - m_all) for st in stats]
    den = None
    for a, st in zip(scale, stats):
        term = a * pltpu.roll(st, LANES - STAT_HALF, axis=1)
        den = term if den is None else den + term
    comb = None
    for a, acc in zip(scale, accs):
        w = jnp.where(on_max_lanes, a / den, 0.0)
        w_hi = w.astype(BF16)
        w_lo = (w - w_hi.astype(F32)).astype(BF16)
        w_full = jnp.dot(jnp.concatenate([w_hi, w_lo], axis=1), expand_ref[...],
                         preferred_element_type=F32)
        term = w_full * acc.astype(F32)
        comb = term if comb is None else comb + term
    x = h_ref[...] + jnp.dot(comb.astype(BF16), wo_ref[...], preferred_element_type=F32)
    y = _mlp_body(x, g_ref, wup_ref, wdn_ref)
    if final_norm:
        y = _rmsnorm(y, gf_ref[...])
    o_ref[...] = y


def _residue_major_spec(width, tiles_per_seq):
    def index(t):
        in_seq = t % tiles_per_seq
        return (t // tiles_per_seq, in_seq // TILES_PER_SUPER, 0, in_seq % TILES_PER_SUPER, 0)
    return pl.BlockSpec((None, None, MAX_DIL, CHUNK, width), index)


def _mlp(h, g, w_up, w_down, g_final, *, seq, attn=None):
    t, d = h.shape
    d_ff = w_up.shape[1]
    row = lambda width: pl.BlockSpec((TM, width), lambda i: (i, 0))
    args = [h]
    in_specs = [row(d)]
    if attn is not None:
        accs, stats, unperm, expand, w_o = attn
        args += [*accs, *stats, unperm, expand, w_o]
        rm = functools.partial(_residue_major_spec, tiles_per_seq=seq // TM)
        in_specs += [row(GROUP_WIDTH), rm(GROUP_WIDTH), rm(GROUP_WIDTH)]
        in_specs += [row(LANES), rm(LANES), rm(LANES)]
        in_specs += [_resident(unperm.shape), _resident(expand.shape), _resident(w_o.shape)]
        body = _attn_mlp_kernel
    else:
        body = _mlp_kernel
    args += [g, w_up, w_down]
    in_specs += [_resident((1, d)), _resident((d, d_ff)), _resident((d_ff, d))]
    if g_final is not None:
        args.append(g_final)
        in_specs.append(_resident((1, d)))
    return pl.pallas_call(
        functools.partial(body, final_norm=g_final is not None),
        grid=(t // TM,),
        in_specs=in_specs,
        out_specs=row(d),
        out_shape=jax.ShapeDtypeStruct((t, d), F32),
        compiler_params=_params(1),
        name="attn_mlp" if attn is not None else "mlp",
    )(*args)


def _proj_kernel(*refs, n_proj, scales):
    h_ref, perm_ref = refs[0], refs[1]
    x = h_ref[...]
    xn = x * lax.rsqrt(jnp.mean(x * x, axis=-1, keepdims=True) + EPS)
    for p in range(n_proj):
        g_ref, wnat_ref, wperm_ref = refs[2 + 3 * p: 5 + 3 * p]
        onat_ref, operm_ref = refs[2 + 3 * n_proj + 2 * p: 4 + 3 * n_proj + 2 * p]
        hn = (xn * g_ref[...]).astype(BF16)
        y = jnp.dot(hn, wnat_ref[...], preferred_element_type=F32)
        onat_ref[...] = (y * scales[p]).astype(onat_ref.dtype)
        hn_p = _to_residue_major(perm_ref[...], hn).astype(BF16).reshape(hn.shape)
        y = jnp.dot(hn_p, wperm_ref[...], preferred_element_type=F32)
        operm_ref[...] = (y * scales[p]).astype(operm_ref.dtype).reshape(operm_ref.shape)


def _proj(h, perm, projs, *, batch, seq):
    t, d = h.shape
    args = [h, perm]
    in_specs = [pl.BlockSpec((TM, d), lambda i: (i, 0)), _resident(perm.shape)]
    out_specs, out_shapes = [], []
    for g, w_nat, w_perm, _ in projs:
        args += [g, w_nat, w_perm]
        in_specs += [_resident((1, d)), _resident(w_nat.shape), _resident(w_perm.shape)]
        out_specs.append(pl.BlockSpec((TM, w_nat.shape[1]), lambda i: (i, 0)))
        out_shapes.append(jax.ShapeDtypeStruct((t, w_nat.shape[1]), BF16))
        out_specs.append(_residue_major_spec(w_perm.shape[1], seq // TM))
        out_shapes.append(jax.ShapeDtypeStruct(
            (batch, seq // SUPER, MAX_DIL, KEY_BLOCK, w_perm.shape[1]), BF16))
    return pl.pallas_call(
        functools.partial(_proj_kernel, n_proj=len(projs), scales=tuple(p[3] for p in projs)),
        grid=(t // TM,),
        in_specs=in_specs,
        out_specs=out_specs,
        out_shape=out_shapes,
        compiler_params=_params(1),
        name="norm_proj",
    )(*args)


def _attn_kernel(q_ref, kc_ref, vc_ref, kp_ref, vp_ref, bias_ref, o_ref, st_ref, *, layout,
                 chains, chain_len, first_axis):
    first = jnp.where(pl.program_id(first_axis) == 0, 1, 0)
    lane = lax.broadcasted_iota(jnp.int32, (KEY_BLOCK, LANES), 1)
    nt = (((1,), (1,)), ((), ()))

    def block(ref, c, j, cols):
        if layout == "rows":
            return ref[j * KEY_BLOCK:(j + 1) * KEY_BLOCK, cols]
        if layout == "chunks":
            return ref[:, c, j, :, cols].reshape(KEY_BLOCK, LANES)
        return ref[c, :, cols]

    def prev_block(ref, c, cols):
        if layout == "rows":
            return ref[:, cols]
        if layout == "chunks":
            return ref[:, c, :, cols].reshape(KEY_BLOCK, LANES)
        return ref[c, :, cols]

    def store(ref, c, j, cols, val):
        if layout == "rows":
            ref[j * KEY_BLOCK:(j + 1) * KEY_BLOCK, cols] = val
        elif layout == "chunks":
            ref[:, c, j, :, cols] = val.reshape(MID_DIL, MID_CHUNK, val.shape[-1])
        else:
            ref[c, :, cols] = val

    for c in range(chains):
        for j in range(chain_len):
            st_tile = jnp.zeros((KEY_BLOCK, LANES), F32)
            for hp in range(HEAD_PAIRS):
                cols = slice(hp * LANES, (hp + 1) * LANES)
                q2 = block(q_ref, c, j, cols)
                if j == 0:
                    k_prev, v_prev = prev_block(kp_ref, c, cols), prev_block(vp_ref, c, cols)
                    variant = first
                else:
                    k_prev, v_prev = block(kc_ref, c, j - 1, cols), block(vc_ref, c, j - 1, cols)
                    variant = 0
                k2 = jnp.concatenate([k_prev, block(kc_ref, c, j, cols)], axis=0)
                v2 = jnp.concatenate([v_prev, block(vc_ref, c, j, cols)], axis=0)
                zero = jnp.zeros_like(q2)
                qs = jnp.concatenate([jnp.where(lane < HEAD_DIM, q2, zero),
                                      jnp.where(lane >= HEAD_DIM, q2, zero)], axis=0)
                s = lax.dot_general(qs, k2, nt, preferred_element_type=F32) + bias_ref[variant, hp]
                m = jnp.max(s, axis=1, keepdims=True)
                p = jnp.exp2(s - m)
                l = jnp.sum(p, axis=1, keepdims=True)
                pv = jnp.dot(p.astype(BF16), v2, preferred_element_type=F32)
                store(o_ref, c, j, cols,
                      jnp.where(lane < HEAD_DIM, pv[:KEY_BLOCK], pv[KEY_BLOCK:]).astype(o_ref.dtype))
                for e in range(HEADS_PER_TILE):
                    h = hp * HEADS_PER_TILE + e
                    rows = slice(e * KEY_BLOCK, (e + 1) * KEY_BLOCK)
                    ml = jnp.where((lane % STAT_LANES_PER_HEAD) < STAT_HALF, m[rows], l[rows])
                    st_tile = jnp.where((lane // STAT_LANES_PER_HEAD) == h, ml, st_tile)
            store(st_ref, c, j, slice(None), st_tile)


def _attention_group(q, kv, bias, *, group, q_col, k_col, batch, seq):
    n_super = seq // SUPER
    if group == 0:
        layout, chains, chain_len = "rows", 1, UNITS_PER_STEP
        rows = chain_len * KEY_BLOCK
        grid = (batch, seq // rows)
        cur = lambda col, w=GROUP_WIDTH: pl.BlockSpec((None, rows, w), lambda b, n: (b, n, col))
        prev = lambda col: pl.BlockSpec(
            (None, KEY_BLOCK, GROUP_WIDTH),
            lambda b, n: (b, jnp.maximum(n * chain_len - 1, 0), col))
        out_specs = [cur(0), cur(0, LANES)]
        o_shape, s_shape = (batch, seq, GROUP_WIDTH), (batch, seq, LANES)
        view = lambda a: a
    elif group == 1:
        layout, chain_len = "chunks", SUPER // PATTERNS[1][0]
        chains = UNITS_PER_STEP // chain_len
        grid = (batch, n_super, MID_DIL // chains)
        shape7 = lambda w: (batch, n_super, MID_DIL, MID_DIL, chain_len, MID_CHUNK, w)
        cur = lambda col, w=GROUP_WIDTH: pl.BlockSpec(
            (None, None, MID_DIL, chains, chain_len, MID_CHUNK, w),
            lambda b, n, r: (b, n, 0, r, 0, 0, col))
        prev = lambda col: pl.BlockSpec(
            (None, None, MID_DIL, chains, None, MID_CHUNK, GROUP_WIDTH),
            lambda b, n, r: (b, jnp.maximum(n - 1, 0), 0, r, chain_len - 1, 0, col))
        out_specs = [cur(0), cur(0, LANES)]
        o_shape, s_shape = shape7(GROUP_WIDTH), shape7(LANES)
        view = lambda a: a.reshape(shape7(a.shape[-1]))
    else:
        layout, chains, chain_len = "blocks", UNITS_PER_STEP, 1
        grid = (batch, n_super, MAX_DIL // chains)
        cur = lambda col, w=GROUP_WIDTH: pl.BlockSpec((None, None, chains, KEY_BLOCK, w),
                                                      lambda b, n, r: (b, n, r, 0, col))
        prev = lambda col: pl.BlockSpec((None, None, chains, KEY_BLOCK, GROUP_WIDTH),
                                        lambda b, n, r: (b, jnp.maximum(n - 1, 0), r, 0, col))
        out_specs = [cur(0), cur(0, LANES)]
        o_shape = (batch, n_super, MAX_DIL, KEY_BLOCK, GROUP_WIDTH)
        s_shape = (batch, n_super, MAX_DIL, KEY_BLOCK, LANES)
        view = lambda a: a
    q, kv = view(q), view(kv)
    acc, stats = pl.pallas_call(
        functools.partial(_attn_kernel, layout=layout, chains=chains, chain_len=chain_len,
                          first_axis=1),
        grid=grid,
        in_specs=[cur(q_col), cur(k_col), cur(k_col + 1), prev(k_col), prev(k_col + 1),
                  _resident(bias.shape)],
        out_specs=out_specs,
        out_shape=[jax.ShapeDtypeStruct(o_shape, BF16), jax.ShapeDtypeStruct(s_shape, F32)],
        compiler_params=_params(len(grid)),
        name=f"dilated_attn_g{group}",
    )(q, kv, kv, kv, kv, bias)
    if group == 0:
        return acc.reshape(batch * seq, GROUP_WIDTH), stats.reshape(batch * seq, LANES)
    rm = lambda a: a.reshape(batch, n_super, MAX_DIL, KEY_BLOCK, a.shape[-1])
    return rm(acc), rm(stats)


def _alibi_table(dilation, order):
    slopes = 2.0 ** (-ALIBI_MAX_BIAS * jnp.arange(1, H_G + 1, dtype=F32) / H_G)
    a = order[:, None]
    c = jnp.concatenate([order, KEY_BLOCK + order])[None, :]
    j = KEY_BLOCK + a - c
    valid = (j >= 0) & (j <= KEY_BLOCK)
    bias = -slopes[:, None, None] * (dilation * j).astype(F32) * LOG2E
    bias = jnp.where(valid[None], bias, NEG_INF)
    no_prev = jnp.where((c >= KEY_BLOCK)[None], bias, NEG_INF)
    both = jnp.stack([bias, no_prev])
    return both.reshape(2, HEAD_PAIRS, HEADS_PER_TILE * KEY_BLOCK, 2 * KEY_BLOCK)


def _head_expand_matrix():
    src = jnp.arange(LANES)[:, None]
    dst = jnp.arange(GROUP_WIDTH)[None, :]
    e = (src == (dst // HEAD_DIM) * STAT_LANES_PER_HEAD).astype(BF16)
    return jnp.concatenate([e, e], axis=0)


def _piece_permutation():
    dst = jnp.arange(PERM_ROWS)
    src = MAX_DIL * (dst % BF16_ROWS) + dst // BF16_ROWS
    return (jnp.arange(PERM_ROWS)[None, :] == src[:, None]).astype(BF16)


def kernel(x, norm_mix, norm_mlp, w_a_in, conv_w, w_a_out, norm_kv, w_kv, w_q, w_o, w_up, w_down,
           norm_final):
    batch, seq, d = x.shape
    depth = norm_mix.shape[0]
    n_a = w_a_in.shape[0]
    assert seq % SUPER == 0 and d % LANES == 0
    assert [w // dil for w, dil in PATTERNS] == [KEY_BLOCK] * N_GROUPS

    bf = lambda w: w.astype(BF16)
    gain = lambda g: g.reshape(1, d).astype(F32)
    h = x.reshape(batch * seq, d)
    expand = _head_expand_matrix()
    perm = _piece_permutation()
    unperm = perm.T
    idx = jnp.arange(KEY_BLOCK)
    orders = [idx, MID_DIL * (idx % MID_CHUNK) + idx // MID_CHUNK, idx]
    tables = [_alibi_table(dil, order) for (_, dil), order in zip(PATTERNS, orders)]
    q_scale = HEAD_DIM ** -0.5 * LOG2E
    kv_nat = kv_rm = None
    for l in range(depth):
        g_final = gain(norm_final) if l == depth - 1 else None
        if l < n_a:
            h = _conv_mixer(h, gain(norm_mix[l]), bf(w_a_in[l]), conv_w[l].astype(F32),
                            bf(w_a_out[l]), seq=seq)
            h = _mlp(h, gain(norm_mlp[l]), bf(w_up[l]), bf(w_down[l]), g_final, seq=seq)
        else:
            i = l - n_a
            wq = bf(w_q[i])
            projs = [(gain(norm_mix[l]), wq[:, :GROUP_WIDTH], wq[:, GROUP_WIDTH:], q_scale)]
            if kv_nat is None:
                wkv = bf(w_kv)
                projs.append((gain(norm_kv), wkv[:, :2 * GROUP_WIDTH], wkv[:, 2 * GROUP_WIDTH:], 1.0))
                q_nat, q_rm, kv_nat, kv_rm = _proj(h, perm, projs, batch=batch, seq=seq)
            else:
                q_nat, q_rm = _proj(h, perm, projs, batch=batch, seq=seq)
            q_nat3 = q_nat.reshape(batch, seq, -1)
            kv_nat3 = kv_nat.reshape(batch, seq, -1)
            accs, stats = [], []
            for g in range(N_GROUPS):
                if g == 0:
                    acc_g, st_g = _attention_group(q_nat3, kv_nat3, tables[g], group=g, q_col=0,
                                                   k_col=0, batch=batch, seq=seq)
                else:
                    acc_g, st_g = _attention_group(q_rm, kv_rm, tables[g], group=g, q_col=g - 1,
                                                   k_col=2 * (g - 1), batch=batch, seq=seq)
                accs.append(acc_g)
                stats.append(st_g)
            h = _mlp(h, gain(norm_mlp[l]), bf(w_up[l]), bf(w_down[l]), g_final, seq=seq,
                     attn=(accs, stats, unperm, expand, bf(w_o[i])))
    return h.reshape(batch, seq, d)
```

```python
import functools

import jax
import jax.numpy as jnp
from jax import lax
from jax.experimental import pallas as pl
from jax.experimental.pallas import tpu as pltpu

PATTERNS = ((128, 1), (512, 4), (2048, 16))
N_GROUPS = len(PATTERNS)
H_G = 8
HEAD_DIM = 64
GROUP_WIDTH = H_G * HEAD_DIM
EPS = 1e-5
ALIBI_MAX_BIAS = 8.0
NEG_INF = -1e30
CONV_WIDTH = 3
LOG2E = 1.4426950408889634

LANES = 128
SUBLANES = 8
BF16_ROWS = 16
KEY_BLOCK = 128
HEADS_PER_TILE = LANES // HEAD_DIM
HEAD_PAIRS = H_G // HEADS_PER_TILE
STAT_LANES_PER_HEAD = LANES // H_G
STAT_HALF = STAT_LANES_PER_HEAD // 2

MAX_DIL = PATTERNS[2][1]
SUPER = PATTERNS[2][0]
MID_DIL = PATTERNS[1][1]
MID_CHUNK = KEY_BLOCK // MID_DIL
TM = 512
CHUNK = TM // MAX_DIL
TILES_PER_SUPER = SUPER // TM
PERM_ROWS = MAX_DIL * BF16_ROWS
PERM_PARTS = TM // PERM_ROWS
VMEM_LIMIT_BYTES = 56 * 1024 * 1024

F32 = jnp.float32
BF16 = jnp.bfloat16


def _params(n_axes):
    return pltpu.CompilerParams(
        dimension_semantics=("arbitrary",) * n_axes,
        vmem_limit_bytes=VMEM_LIMIT_BYTES,
    )


def _resident(shape):
    zeros = (0,) * len(shape)
    return pl.BlockSpec(shape, lambda *_: zeros, pipeline_mode=pl.Buffered(1))


def _rmsnorm(x, g):
    ms = jnp.mean(x * x, axis=-1, keepdims=True)
    return x * lax.rsqrt(ms + EPS) * g


def _split3(x):
    hi = x.astype(BF16)
    r = x - hi.astype(F32)
    mid = r.astype(BF16)
    lo = (r - mid.astype(F32)).astype(BF16)
    return hi, mid, lo


def _to_residue_major(perm, x):
    parts = [jnp.dot(perm, x[s * PERM_ROWS:(s + 1) * PERM_ROWS], preferred_element_type=F32)
             .reshape(MAX_DIL, BF16_ROWS, x.shape[-1]) for s in range(PERM_PARTS)]
    return jnp.concatenate(parts, axis=1)


def _to_natural(unperm, x):
    parts = [jnp.dot(unperm, x[:, s * BF16_ROWS:(s + 1) * BF16_ROWS, :].reshape(PERM_ROWS, x.shape[-1]),
                     preferred_element_type=F32) for s in range(PERM_PARTS)]
    return jnp.concatenate(parts, axis=0)


def _conv_mixer_kernel(h_ref, g_ref, win_ref, cw_ref, wout_ref, o_ref, cu_ref, *, tiles_per_seq):
    tm, d = h_ref.shape
    x = h_ref[...]
    hn = _rmsnorm(x, g_ref[...]).astype(BF16)
    c_u = jnp.dot(hn, win_ref[:, d:], preferred_element_type=F32)
    cu = c_u[:, :d] * c_u[:, d:]

    @pl.when(pl.program_id(0) % tiles_per_seq == 0)
    def _():
        cu_ref[0:SUBLANES, :] = jnp.zeros((SUBLANES, d), F32)

    cu_ref[SUBLANES:SUBLANES + tm, :] = cu
    cw = cw_ref[...]
    conv = cw[0:1, :] * cu
    for k in range(1, CONV_WIDTH):
        conv = conv + cw[k:k + 1, :] * cu_ref[SUBLANES - k:SUBLANES - k + tm, :]
    b = jnp.dot(hn, win_ref[:, :d], preferred_element_type=F32)
    y = jnp.dot((b * conv).astype(BF16), wout_ref[...], preferred_element_type=F32)
    o_ref[...] = x + y
    cu_ref[0:SUBLANES, :] = cu_ref[tm:tm + SUBLANES, :]


def _conv_mixer(h, g, w_in, conv_w, w_out, *, seq):
    t, d = h.shape
    return pl.pallas_call(
        functools.partial(_conv_mixer_kernel, tiles_per_seq=seq // TM),
        grid=(t // TM,),
        in_specs=[
            pl.BlockSpec((TM, d), lambda i: (i, 0)),
            _resident((1, d)),
            _resident((d, 3 * d)),
            _resident((CONV_WIDTH, d)),
            _resident((d, d)),
        ],
        out_specs=pl.BlockSpec((TM, d), lambda i: (i, 0)),
        out_shape=jax.ShapeDtypeStruct((t, d), F32),
        scratch_shapes=[pltpu.VMEM((TM + SUBLANES, d), F32)],
        compiler_params=_params(1),
        name="conv_mixer",
    )(h, g, w_in, conv_w, w_out)


def _mlp_body(x, g_ref, wup_ref, wdn_ref):
    hn = _rmsnorm(x, g_ref[...]).astype(BF16)
    a = jnp.dot(hn, wup_ref[...], preferred_element_type=F32)
    a = jnp.maximum(a, 0.0)
    a = (a * a).astype(BF16)
    return x + jnp.dot(a, wdn_ref[...], preferred_element_type=F32)


def _mlp_kernel(h_ref, g_ref, wup_ref, wdn_ref, *rest, final_norm):
    if final_norm:
        gf_ref, o_ref = rest
    else:
        (o_ref,) = rest
    y = _mlp_body(h_ref[...], g_ref, wup_ref, wdn_ref)
    if final_norm:
        y = _rmsnorm(y, gf_ref[...])
    o_ref[...] = y


def _attn_mlp_kernel(h_ref, o0_ref, o1_ref, o2_ref, s0_ref, s1_ref, s2_ref, unperm_ref, expand_ref,
                     wo_ref, g_ref, wup_ref, wdn_ref, *rest, final_norm):
    if final_norm:
        gf_ref, o_ref = rest
    else:
        (o_ref,) = rest
    tm = h_ref.shape[0]
    unperm = unperm_ref[...]
    accs = [o0_ref[...]]
    stats = [s0_ref[...]]
    for og_ref, sg_ref in ((o1_ref, s1_ref), (o2_ref, s2_ref)):
        accs.append(_to_natural(unperm, og_ref[...]).astype(BF16))
        parts = jnp.concatenate(_split3(sg_ref[...]), axis=2)
        moved = _to_natural(unperm, parts)
        stats.append(moved[:, :LANES] + moved[:, LANES:2 * LANES] + moved[:, 2 * LANES:])
    lane = lax.broadcasted_iota(jnp.int32, (tm, LANES), 1)
    on_max_lanes = (lane % STAT_LANES_PER_HEAD) < STAT_HALF
    m_all = jnp.maximum(jnp.maximum(stats[0], stats[1]), stats[2])
    scale = [jnp.exp2(st - m_all) for st in stats]
    den = None
    for a, st in zip(scale, stats):
        term = a * pltpu.roll(st, LANES - STAT_HALF, axis=1)
        den = term if den is None else den + term
    comb = None
    for a, acc in zip(scale, accs):
        w = jnp.where(on_max_lanes, a / den, 0.0)
        w_hi = w.astype(BF16)
        w_lo = (w - w_hi.astype(F32)).astype(BF16)
        w_full = jnp.dot(jnp.concatenate([w_hi, w_lo], axis=1), expand_ref[...],
                         preferred_element_type=F32)
        term = w_full * acc.astype(F32)
        comb = term if comb is None else comb + term
    x = h_ref[...] + jnp.dot(comb.astype(BF16), wo_ref[...], preferred_element_type=F32)
    y = _mlp_body(x, g_ref, wup_ref, wdn_ref)
    if final_norm:
        y = _rmsnorm(y, gf_ref[...])
    o_ref[...] = y


def _residue_major_spec(width, tiles_per_seq):
    def index(t):
        in_seq = t % tiles_per_seq
        return (t // tiles_per_seq, in_seq // TILES_PER_SUPER, 0, in_seq % TILES_PER_SUPER, 0)
    return pl.BlockSpec((None, None, MAX_DIL, CHUNK, width), index)


def _mlp(h, g, w_up, w_down, g_final, *, seq, attn=None):
    t, d = h.shape
    d_ff = w_up.shape[1]
    row = lambda width: pl.BlockSpec((TM, width), lambda i: (i, 0))
    args = [h]
    in_specs = [row(d)]
    if attn is not None:
        accs, stats, unperm, expand, w_o = attn
        args += [*accs, *stats, unperm, expand, w_o]
        rm = functools.partial(_residue_major_spec, tiles_per_seq=seq // TM)
        in_specs += [row(GROUP_WIDTH), rm(GROUP_WIDTH), rm(GROUP_WIDTH)]
        in_specs += [row(LANES), rm(LANES), rm(LANES)]
        in_specs += [_resident(unperm.shape), _resident(expand.shape), _resident(w_o.shape)]
        body = _attn_mlp_kernel
    else:
        body = _mlp_kernel
    args += [g, w_up, w_down]
    in_specs += [_resident((1, d)), _resident((d, d_ff)), _resident((d_ff, d))]
    if g_final is not None:
        args.append(g_final)
        in_specs.append(_resident((1, d)))
    return pl.pallas_call(
        functools.partial(body, final_norm=g_final is not None),
        grid=(t // TM,),
        in_specs=in_specs,
        out_specs=row(d),
        out_shape=jax.ShapeDtypeStruct((t, d), F32),
        compiler_params=_params(1),
        name="attn_mlp" if attn is not None else "mlp",
    )(*args)


def _proj_kernel(*refs, n_proj, scales):
    h_ref, perm_ref = refs[0], refs[1]
    x = h_ref[...]
    xn = x * lax.rsqrt(jnp.mean(x * x, axis=-1, keepdims=True) + EPS)
    for p in range(n_proj):
        g_ref, wnat_ref, wperm_ref = refs[2 + 3 * p: 5 + 3 * p]
        onat_ref, operm_ref = refs[2 + 3 * n_proj + 2 * p: 4 + 3 * n_proj + 2 * p]
        hn = (xn * g_ref[...]).astype(BF16)
        y = jnp.dot(hn, wnat_ref[...], preferred_element_type=F32)
        onat_ref[...] = (y * scales[p]).astype(onat_ref.dtype)
        hn_p = _to_residue_major(perm_ref[...], hn).astype(BF16).reshape(hn.shape)
        y = jnp.dot(hn_p, wperm_ref[...], preferred_element_type=F32)
        operm_ref[...] = (y * scales[p]).astype(operm_ref.dtype).reshape(operm_ref.shape)


def _proj(h, perm, projs, *, batch, seq):
    t, d = h.shape
    args = [h, perm]
    in_specs = [pl.BlockSpec((TM, d), lambda i: (i, 0)), _resident(perm.shape)]
    out_specs, out_shapes = [], []
    for g, w_nat, w_perm, _ in projs:
        args += [g, w_nat, w_perm]
        in_specs += [_resident((1, d)), _resident(w_nat.shape), _resident(w_perm.shape)]
        out_specs.append(pl.BlockSpec((TM, w_nat.shape[1]), lambda i: (i, 0)))
        out_shapes.append(jax.ShapeDtypeStruct((t, w_nat.shape[1]), BF16))
        out_specs.append(_residue_major_spec(w_perm.shape[1], seq // TM))
        out_shapes.append(jax.ShapeDtypeStruct(
            (batch, seq // SUPER, MAX_DIL, KEY_BLOCK, w_perm.shape[1]), BF16))
    return pl.pallas_call(
        functools.partial(_proj_kernel, n_proj=len(projs), scales=tuple(p[3] for p in projs)),
        grid=(t // TM,),
        in_specs=in_specs,
        out_specs=out_specs,
        out_shape=out_shapes,
        compiler_params=_params(1),
        name="norm_proj",
    )(*args)


def _attn_kernel(q_ref, k_ref, v_ref, bias_ref, o_ref, st_ref, *, layout, chains, chain_len):
    lane = lax.broadcasted_iota(jnp.int32, (KEY_BLOCK, LANES), 1)
    nt = (((1,), (1,)), ((), ()))
    blocks_per_super = SUPER // PATTERNS[1][0]
    ones = jnp.ones((2 * KEY_BLOCK, LANES), BF16)

    def index(c, j, cols):
        if layout == "rows":
            return (slice(j * KEY_BLOCK, (j + 1) * KEY_BLOCK), cols)
        if layout == "chunks":
            return (j // blocks_per_super, slice(None), c, j % blocks_per_super, slice(None), cols)
        return (j, c, slice(None), cols)

    def block(ref, c, j, cols):
        return ref[index(c, j, cols)].reshape(KEY_BLOCK, LANES)

    def store(ref, c, j, cols, val):
        idx = index(c, j, cols)
        if layout == "chunks":
            val = val.reshape(MID_DIL, MID_CHUNK, val.shape[-1])
        ref[idx] = val

    for c in range(chains):
        for j in range(chain_len):
            st_tile = jnp.zeros((KEY_BLOCK, LANES), F32)
            for hp in range(HEAD_PAIRS):
                cols = slice(hp * LANES, (hp + 1) * LANES)
                q2 = block(q_ref, c, j, cols)
                k_cur, v_cur = block(k_ref, c, j, cols), block(v_ref, c, j, cols)
                if j == 0:
                    k_prev, v_prev = k_cur, v_cur
                else:
                    k_prev, v_prev = block(k_ref, c, j - 1, cols), block(v_ref, c, j - 1, cols)
                k2 = jnp.concatenate([k_prev, k_cur], axis=0)
                v2 = jnp.concatenate([jnp.concatenate([v_prev, v_cur], axis=0), ones], axis=1)
                zero = jnp.zeros_like(q2)
                qs = jnp.concatenate([jnp.where(lane < HEAD_DIM, q2, zero),
                                      jnp.where(lane >= HEAD_DIM, q2, zero)], axis=0)
                s = lax.dot_general(qs, k2, nt, preferred_element_type=F32)
                s = s + bias_ref[1 if j == 0 else 0, hp]
                m = jnp.max(s, axis=1, keepdims=True)
                p = jnp.exp2(s - m).astype(BF16)
                pv = jnp.dot(p, v2, preferred_element_type=F32)
                acc, l = pv[:, :LANES], pv[:, LANES:]
                store(o_ref, c, j, cols,
                      jnp.where(lane < HEAD_DIM, acc[:KEY_BLOCK], acc[KEY_BLOCK:]).astype(o_ref.dtype))
                for e in range(HEADS_PER_TILE):
                    h = hp * HEADS_PER_TILE + e
                    rows = slice(e * KEY_BLOCK, (e + 1) * KEY_BLOCK)
                    ml = jnp.where((lane % STAT_LANES_PER_HEAD) < STAT_HALF, m[rows], l[rows])
                    st_tile = jnp.where((lane // STAT_LANES_PER_HEAD) == h, ml, st_tile)
            store(st_ref, c, j, slice(None), st_tile)


def _attention_group(q, kv, bias, *, group, q_col, k_col, batch, seq):
    n_super = seq // SUPER
    dil = PATTERNS[group][1]
    chains, chain_len = dil, seq // (dil * KEY_BLOCK)
    if group == 0:
        layout = "rows"
        shape = lambda w: (batch, seq, w)
    elif group == 1:
        layout = "chunks"
        shape = lambda w: (batch, n_super, MID_DIL, MID_DIL, SUPER // PATTERNS[1][0], MID_CHUNK, w)
    else:
        layout = "blocks"
        shape = lambda w: (batch, n_super, MAX_DIL, KEY_BLOCK, w)
    n_lead = len(shape(1)) - 2

    def spec(col, w=GROUP_WIDTH):
        return pl.BlockSpec((None,) + shape(w)[1:-1] + (w,), lambda b: (b,) + (0,) * n_lead + (col,))

    q, kv = q.reshape(shape(q.shape[-1])), kv.reshape(shape(kv.shape[-1]))
    acc, stats = pl.pallas_call(
        functools.partial(_attn_kernel, layout=layout, chains=chains, chain_len=chain_len),
        grid=(batch,),
        in_specs=[spec(q_col), spec(k_col), spec(k_col + 1), _resident(bias.shape)],
        out_specs=[spec(0), spec(0, LANES)],
        out_shape=[jax.ShapeDtypeStruct(shape(GROUP_WIDTH), BF16),
                   jax.ShapeDtypeStruct(shape(LANES), F32)],
        compiler_params=_params(1),
        name=f"dilated_attn_g{group}",
    )(q, kv, kv, bias)
    if group == 0:
        return acc.reshape(batch * seq, GROUP_WIDTH), stats.reshape(batch * seq, LANES)
    rm = lambda a: a.reshape(batch, n_super, MAX_DIL, KEY_BLOCK, a.shape[-1])
    return rm(acc), rm(stats)


def _alibi_table(dilation, order):
    slopes = 2.0 ** (-ALIBI_MAX_BIAS * jnp.arange(1, H_G + 1, dtype=F32) / H_G)
    a = order[:, None]
    c = jnp.concatenate([order, KEY_BLOCK + order])[None, :]
    j = KEY_BLOCK + a - c
    valid = (j >= 0) & (j <= KEY_BLOCK)
    bias = -slopes[:, None, None] * (dilation * j).astype(F32) * LOG2E
    bias = jnp.where(valid[None], bias, NEG_INF)
    no_prev = jnp.where((c >= KEY_BLOCK)[None], bias, NEG_INF)
    both = jnp.stack([bias, no_prev])
    return both.reshape(2, HEAD_PAIRS, HEADS_PER_TILE * KEY_BLOCK, 2 * KEY_BLOCK)


def _head_expand_matrix():
    src = jnp.arange(LANES)[:, None]
    dst = jnp.arange(GROUP_WIDTH)[None, :]
    e = (src == (dst // HEAD_DIM) * STAT_LANES_PER_HEAD).astype(BF16)
    return jnp.concatenate([e, e], axis=0)


def _piece_permutation():
    dst = jnp.arange(PERM_ROWS)
    src = MAX_DIL * (dst % BF16_ROWS) + dst // BF16_ROWS
    return (jnp.arange(PERM_ROWS)[None, :] == src[:, None]).astype(BF16)


def kernel(x, norm_mix, norm_mlp, w_a_in, conv_w, w_a_out, norm_kv, w_kv, w_q, w_o, w_up, w_down,
           norm_final):
    batch, seq, d = x.shape
    depth = norm_mix.shape[0]
    n_a = w_a_in.shape[0]
    assert seq % SUPER == 0 and d % LANES == 0
    assert [w // dil for w, dil in PATTERNS] == [KEY_BLOCK] * N_GROUPS

    bf = lambda w: w.astype(BF16)
    gain = lambda g: g.reshape(1, d).astype(F32)
    h = x.reshape(batch * seq, d)
    expand = _head_expand_matrix()
    perm = _piece_permutation()
    unperm = perm.T
    idx = jnp.arange(KEY_BLOCK)
    orders = [idx, MID_DIL * (idx % MID_CHUNK) + idx // MID_CHUNK, idx]
    tables = [_alibi_table(dil, order) for (_, dil), order in zip(PATTERNS, orders)]
    q_scale = HEAD_DIM ** -0.5 * LOG2E
    kv_nat = kv_rm = None
    for l in range(depth):
        g_final = gain(norm_final) if l == depth - 1 else None
        if l < n_a:
            h = _conv_mixer(h, gain(norm_mix[l]), bf(w_a_in[l]), conv_w[l].astype(F32),
                            bf(w_a_out[l]), seq=seq)
            h = _mlp(h, gain(norm_mlp[l]), bf(w_up[l]), bf(w_down[l]), g_final, seq=seq)
        else:
            i = l - n_a
            wq = bf(w_q[i])
            projs = [(gain(norm_mix[l]), wq[:, :GROUP_WIDTH], wq[:, GROUP_WIDTH:], q_scale)]
            if kv_nat is None:
                wkv = bf(w_kv)
                projs.append((gain(norm_kv), wkv[:, :2 * GROUP_WIDTH], wkv[:, 2 * GROUP_WIDTH:], 1.0))
                q_nat, q_rm, kv_nat, kv_rm = _proj(h, perm, projs, batch=batch, seq=seq)
            else:
                q_nat, q_rm = _proj(h, perm, projs, batch=batch, seq=seq)
            q_nat3 = q_nat.reshape(batch, seq, -1)
            kv_nat3 = kv_nat.reshape(batch, seq, -1)
            accs, stats = [], []
            for g in range(N_GROUPS):
                if g == 0:
                    acc_g, st_g = _attention_group(q_nat3, kv_nat3, tables[g], group=g, q_col=0,
                                                   k_col=0, batch=batch, seq=seq)
                else:
                    acc_g, st_g = _attention_group(q_rm, kv_rm, tables[g], group=g, q_col=g - 1,
                                                   k_col=2 * (g - 1), batch=batch, seq=seq)
                accs.append(acc_g)
                stats.append(st_g)
            h = _mlp(h, gain(norm_mlp[l]), bf(w_up[l]), bf(w_down[l]), g_final, seq=seq,
                     attn=(accs, stats, unperm, expand, bf(w_o[i])))
    return h.reshape(batch, seq, d)
```

```python
import functools

import jax
import jax.numpy as jnp
from jax import lax
from jax.experimental import pallas as pl
from jax.experimental.pallas import tpu as pltpu

PATTERNS = ((128, 1), (512, 4), (2048, 16))
N_GROUPS = len(PATTERNS)
H_G = 8
HEAD_DIM = 64
GROUP_WIDTH = H_G * HEAD_DIM
EPS = 1e-5
ALIBI_MAX_BIAS = 8.0
NEG_INF = -1e30
CONV_WIDTH = 3
LOG2E = 1.4426950408889634

LANES = 128
SUBLANES = 8
BF16_ROWS = 16
KEY_BLOCK = 128
HEADS_PER_TILE = LANES // HEAD_DIM
HEAD_PAIRS = H_G // HEADS_PER_TILE
STAT_LANES_PER_HEAD = LANES // H_G
STAT_HALF = STAT_LANES_PER_HEAD // 2

MAX_DIL = PATTERNS[2][1]
SUPER = PATTERNS[2][0]
MID_DIL = PATTERNS[1][1]
MID_CHUNK = KEY_BLOCK // MID_DIL
TM = 512
TM_WIDE = 1024
PERM_ROWS = MAX_DIL * BF16_ROWS
VMEM_LIMIT_BYTES = 56 * 1024 * 1024

F32 = jnp.float32
BF16 = jnp.bfloat16


def _params(n_axes):
    return pltpu.CompilerParams(
        dimension_semantics=("arbitrary",) * n_axes,
        vmem_limit_bytes=VMEM_LIMIT_BYTES,
    )


def _resident(shape):
    zeros = (0,) * len(shape)
    return pl.BlockSpec(shape, lambda *_: zeros, pipeline_mode=pl.Buffered(1))


def _rmsnorm(x, g):
    ms = jnp.mean(x * x, axis=-1, keepdims=True)
    return x * lax.rsqrt(ms + EPS) * g


def _split3(x):
    hi = x.astype(BF16)
    r = x - hi.astype(F32)
    mid = r.astype(BF16)
    lo = (r - mid.astype(F32)).astype(BF16)
    return hi, mid, lo


def _to_residue_major(perm, x):
    parts = [jnp.dot(perm, x[s * PERM_ROWS:(s + 1) * PERM_ROWS], preferred_element_type=F32)
             .reshape(MAX_DIL, BF16_ROWS, x.shape[-1]) for s in range(x.shape[0] // PERM_ROWS)]
    return jnp.concatenate(parts, axis=1)


def _to_natural(unperm, x):
    parts = [jnp.dot(unperm, x[:, s * BF16_ROWS:(s + 1) * BF16_ROWS, :].reshape(PERM_ROWS, x.shape[-1]),
                     preferred_element_type=F32) for s in range(x.shape[1] // BF16_ROWS)]
    return jnp.concatenate(parts, axis=0)


def _conv_mixer_kernel(h_ref, g_ref, win_ref, cw_ref, wout_ref, o_ref, cu_ref, *, tiles_per_seq):
    tm, d = h_ref.shape
    x = h_ref[...]
    hn = _rmsnorm(x, g_ref[...]).astype(BF16)
    c_u = jnp.dot(hn, win_ref[:, d:], preferred_element_type=F32)
    cu = c_u[:, :d] * c_u[:, d:]

    @pl.when(pl.program_id(0) % tiles_per_seq == 0)
    def _():
        cu_ref[0:SUBLANES, :] = jnp.zeros((SUBLANES, d), F32)

    cu_ref[SUBLANES:SUBLANES + tm, :] = cu
    cw = cw_ref[...]
    conv = cw[0:1, :] * cu
    for k in range(1, CONV_WIDTH):
        conv = conv + cw[k:k + 1, :] * cu_ref[SUBLANES - k:SUBLANES - k + tm, :]
    b = jnp.dot(hn, win_ref[:, :d], preferred_element_type=F32)
    y = jnp.dot((b * conv).astype(BF16), wout_ref[...], preferred_element_type=F32)
    o_ref[...] = x + y
    cu_ref[0:SUBLANES, :] = cu_ref[tm:tm + SUBLANES, :]


def _conv_mixer(h, g, w_in, conv_w, w_out, *, seq):
    t, d = h.shape
    return pl.pallas_call(
        functools.partial(_conv_mixer_kernel, tiles_per_seq=seq // TM_WIDE),
        grid=(t // TM_WIDE,),
        in_specs=[
            pl.BlockSpec((TM_WIDE, d), lambda i: (i, 0)),
            _resident((1, d)),
            _resident((d, 3 * d)),
            _resident((CONV_WIDTH, d)),
            _resident((d, d)),
        ],
        out_specs=pl.BlockSpec((TM_WIDE, d), lambda i: (i, 0)),
        out_shape=jax.ShapeDtypeStruct((t, d), F32),
        scratch_shapes=[pltpu.VMEM((TM_WIDE + SUBLANES, d), F32)],
        compiler_params=_params(1),
        name="conv_mixer",
    )(h, g, w_in, conv_w, w_out)


def _mlp_body(x, g_ref, wup_ref, wdn_ref):
    hn = _rmsnorm(x, g_ref[...]).astype(BF16)
    a = jnp.dot(hn, wup_ref[...], preferred_element_type=F32)
    a = jnp.maximum(a, 0.0)
    a = (a * a).astype(BF16)
    return x + jnp.dot(a, wdn_ref[...], preferred_element_type=F32)


def _mlp_kernel(h_ref, g_ref, wup_ref, wdn_ref, *rest, final_norm):
    if final_norm:
        gf_ref, o_ref = rest
    else:
        (o_ref,) = rest
    y = _mlp_body(h_ref[...], g_ref, wup_ref, wdn_ref)
    if final_norm:
        y = _rmsnorm(y, gf_ref[...])
    o_ref[...] = y


def _attn_mlp_kernel(h_ref, o0_ref, o1_ref, o2_ref, s0_ref, s1_ref, s2_ref, unperm_ref, expand_ref,
                     wo_ref, g_ref, wup_ref, wdn_ref, *rest, final_norm):
    if final_norm:
        gf_ref, o_ref = rest
    else:
        (o_ref,) = rest
    tm = h_ref.shape[0]
    unperm = unperm_ref[...]
    accs = [o0_ref[...]]
    stats = [s0_ref[...]]
    for og_ref, sg_ref in ((o1_ref, s1_ref), (o2_ref, s2_ref)):
        accs.append(_to_natural(unperm, og_ref[...]).astype(BF16))
        parts = jnp.concatenate(_split3(sg_ref[...]), axis=2)
        moved = _to_natural(unperm, parts)
        stats.append(moved[:, :LANES] + moved[:, LANES:2 * LANES] + moved[:, 2 * LANES:])
    lane = lax.broadcasted_iota(jnp.int32, (tm, LANES), 1)
    on_max_lanes = (lane % STAT_LANES_PER_HEAD) < STAT_HALF
    m_all = jnp.maximum(jnp.maximum(stats[0], stats[1]), stats[2])
    scale = [jnp.exp2(st - m_all) for st in stats]
    den = None
    for a, st in zip(scale, stats):
        term = a * pltpu.roll(st, LANES - STAT_HALF, axis=1)
        den = term if den is None else den + term
    comb = None
    for a, acc in zip(scale, accs):
        w = jnp.where(on_max_lanes, a / den, 0.0)
        w_hi = w.astype(BF16)
        w_lo = (w - w_hi.astype(F32)).astype(BF16)
        w_full = jnp.dot(jnp.concatenate([w_hi, w_lo], axis=1), expand_ref[...],
                         preferred_element_type=F32)
        term = w_full * acc.astype(F32)
        comb = term if comb is None else comb + term
    x = h_ref[...] + jnp.dot(comb.astype(BF16), wo_ref[...], preferred_element_type=F32)
    y = _mlp_body(x, g_ref, wup_ref, wdn_ref)
    if final_norm:
        y = _rmsnorm(y, gf_ref[...])
    o_ref[...] = y


def _residue_major_spec(width, tm, seq):
    tiles_per_seq, tiles_per_super = seq // tm, SUPER // tm

    def index(t):
        in_seq = t % tiles_per_seq
        return (t // tiles_per_seq, in_seq // tiles_per_super, 0, in_seq % tiles_per_super, 0)
    return pl.BlockSpec((None, None, MAX_DIL, tm // MAX_DIL, width), index)


def _mlp(h, g, w_up, w_down, g_final, *, seq, attn=None):
    t, d = h.shape
    d_ff = w_up.shape[1]
    row = lambda width: pl.BlockSpec((TM, width), lambda i: (i, 0))
    args = [h]
    in_specs = [row(d)]
    if attn is not None:
        accs, stats, unperm, expand, w_o = attn
        args += [*accs, *stats, unperm, expand, w_o]
        rm = functools.partial(_residue_major_spec, tm=TM, seq=seq)
        in_specs += [row(GROUP_WIDTH), rm(GROUP_WIDTH), rm(GROUP_WIDTH)]
        in_specs += [row(LANES), rm(LANES), rm(LANES)]
        in_specs += [_resident(unperm.shape), _resident(expand.shape), _resident(w_o.shape)]
        body = _attn_mlp_kernel
    else:
        body = _mlp_kernel
    args += [g, w_up, w_down]
    in_specs += [_resident((1, d)), _resident((d, d_ff)), _resident((d_ff, d))]
    if g_final is not None:
        args.append(g_final)
        in_specs.append(_resident((1, d)))
    return pl.pallas_call(
        functools.partial(body, final_norm=g_final is not None),
        grid=(t // TM,),
        in_specs=in_specs,
        out_specs=row(d),
        out_shape=jax.ShapeDtypeStruct((t, d), F32),
        compiler_params=_params(1),
        name="attn_mlp" if attn is not None else "mlp",
    )(*args)


def _proj_kernel(*refs, n_proj, scales):
    h_ref, perm_ref = refs[0], refs[1]
    x = h_ref[...]
    xn = x * lax.rsqrt(jnp.mean(x * x, axis=-1, keepdims=True) + EPS)
    for p in range(n_proj):
        g_ref, w_ref = refs[2 + 2 * p: 4 + 2 * p]
        onat_ref, operm_ref = refs[2 + 2 * n_proj + 2 * p: 4 + 2 * n_proj + 2 * p]
        n_nat = onat_ref.shape[-1]
        hn = (xn * g_ref[...]).astype(BF16)
        y = jnp.dot(hn, w_ref[:, :n_nat], preferred_element_type=F32)
        onat_ref[...] = (y * scales[p]).astype(onat_ref.dtype)
        hn_p = _to_residue_major(perm_ref[...], hn).astype(BF16).reshape(hn.shape)
        y = jnp.dot(hn_p, w_ref[:, n_nat:], preferred_element_type=F32)
        operm_ref[...] = (y * scales[p]).astype(operm_ref.dtype).reshape(operm_ref.shape)


def _proj(h, perm, projs, *, batch, seq):
    t, d = h.shape
    args = [h, perm]
    in_specs = [pl.BlockSpec((TM_WIDE, d), lambda i: (i, 0)), _resident(perm.shape)]
    out_specs, out_shapes = [], []
    for g, w, n_nat, _ in projs:
        args += [g, w]
        in_specs += [_resident((1, d)), _resident(w.shape)]
        n_perm = w.shape[1] - n_nat
        out_specs.append(pl.BlockSpec((TM_WIDE, n_nat), lambda i: (i, 0)))
        out_shapes.append(jax.ShapeDtypeStruct((t, n_nat), BF16))
        out_specs.append(_residue_major_spec(n_perm, TM_WIDE, seq))
        out_shapes.append(jax.ShapeDtypeStruct(
            (batch, seq // SUPER, MAX_DIL, KEY_BLOCK, n_perm), BF16))
    return pl.pallas_call(
        functools.partial(_proj_kernel, n_proj=len(projs), scales=tuple(p[3] for p in projs)),
        grid=(t // TM_WIDE,),
        in_specs=in_specs,
        out_specs=out_specs,
        out_shape=out_shapes,
        compiler_params=_params(1),
        name="norm_proj",
    )(*args)


def _attn_kernel(q_ref, k_ref, v_ref, bias_ref, o_ref, st_ref, *, layout, chains, chain_len):
    lane = lax.broadcasted_iota(jnp.int32, (KEY_BLOCK, LANES), 1)
    nt = (((1,), (1,)), ((), ()))
    blocks_per_super = SUPER // PATTERNS[1][0]
    ones = jnp.ones((2 * KEY_BLOCK, LANES), BF16)

    def index(c, j, cols):
        if layout == "rows":
            return (slice(j * KEY_BLOCK, (j + 1) * KEY_BLOCK), cols)
        if layout == "chunks":
            return (j // blocks_per_super, slice(None), c, j % blocks_per_super, slice(None), cols)
        return (j, c, slice(None), cols)

    def block(ref, c, j, cols):
        return ref[index(c, j, cols)].reshape(KEY_BLOCK, LANES)

    def store(ref, c, j, cols, val):
        idx = index(c, j, cols)
        if layout == "chunks":
            val = val.reshape(MID_DIL, MID_CHUNK, val.shape[-1])
        ref[idx] = val

    for c in range(chains):
        for j in range(chain_len):
            st_tile = jnp.zeros((KEY_BLOCK, LANES), F32)
            for hp in range(HEAD_PAIRS):
                cols = slice(hp * LANES, (hp + 1) * LANES)
                q2 = block(q_ref, c, j, cols)
                k_cur, v_cur = block(k_ref, c, j, cols), block(v_ref, c, j, cols)
                if j == 0:
                    k_prev, v_prev = k_cur, v_cur
                else:
                    k_prev, v_prev = block(k_ref, c, j - 1, cols), block(v_ref, c, j - 1, cols)
                k2 = jnp.concatenate([k_prev, k_cur], axis=0)
                v2 = jnp.concatenate([jnp.concatenate([v_prev, v_cur], axis=0), ones], axis=1)
                zero = jnp.zeros_like(q2)
                qs = jnp.concatenate([jnp.where(lane < HEAD_DIM, q2, zero),
                                      jnp.where(lane >= HEAD_DIM, q2, zero)], axis=0)
                s = lax.dot_general(qs, k2, nt, preferred_element_type=F32)
                s = s + bias_ref[1 if j == 0 else 0, hp]
                m = jnp.max(s, axis=1, keepdims=True)
                p = jnp.exp2(s - m).astype(BF16)
                pv = jnp.dot(p, v2, preferred_element_type=F32)
                acc, l = pv[:, :LANES], pv[:, LANES:]
                store(o_ref, c, j, cols,
                      jnp.where(lane < HEAD_DIM, acc[:KEY_BLOCK], acc[KEY_BLOCK:]).astype(o_ref.dtype))
                for e in range(HEADS_PER_TILE):
                    h = hp * HEADS_PER_TILE + e
                    rows = slice(e * KEY_BLOCK, (e + 1) * KEY_BLOCK)
                    ml = jnp.where((lane % STAT_LANES_PER_HEAD) < STAT_HALF, m[rows], l[rows])
                    st_tile = jnp.where((lane // STAT_LANES_PER_HEAD) == h, ml, st_tile)
            store(st_ref, c, j, slice(None), st_tile)


def _attention_group(q, kv, bias, *, group, q_col, k_col, batch, seq):
    n_super = seq // SUPER
    dil = PATTERNS[group][1]
    chains, chain_len = dil, seq // (dil * KEY_BLOCK)
    if group == 0:
        layout = "rows"
        shape = lambda w: (batch, seq, w)
    elif group == 1:
        layout = "chunks"
        shape = lambda w: (batch, n_super, MID_DIL, MID_DIL, SUPER // PATTERNS[1][0], MID_CHUNK, w)
    else:
        layout = "blocks"
        shape = lambda w: (batch, n_super, MAX_DIL, KEY_BLOCK, w)
    n_lead = len(shape(1)) - 2

    def spec(col, w=GROUP_WIDTH):
        return pl.BlockSpec((None,) + shape(w)[1:-1] + (w,), lambda b: (b,) + (0,) * n_lead + (col,))

    q, kv = q.reshape(shape(q.shape[-1])), kv.reshape(shape(kv.shape[-1]))
    acc, stats = pl.pallas_call(
        functools.partial(_attn_kernel, layout=layout, chains=chains, chain_len=chain_len),
        grid=(batch,),
        in_specs=[spec(q_col), spec(k_col), spec(k_col + 1), _resident(bias.shape)],
        out_specs=[spec(0), spec(0, LANES)],
        out_shape=[jax.ShapeDtypeStruct(shape(GROUP_WIDTH), BF16),
                   jax.ShapeDtypeStruct(shape(LANES), F32)],
        compiler_params=_params(1),
        name=f"dilated_attn_g{group}",
    )(q, kv, kv, bias)
    if group == 0:
        return acc.reshape(batch * seq, GROUP_WIDTH), stats.reshape(batch * seq, LANES)
    rm = lambda a: a.reshape(batch, n_super, MAX_DIL, KEY_BLOCK, a.shape[-1])
    return rm(acc), rm(stats)


def _alibi_table(dilation, order):
    slopes = 2.0 ** (-ALIBI_MAX_BIAS * jnp.arange(1, H_G + 1, dtype=F32) / H_G)
    a = order[:, None]
    c = jnp.concatenate([order, KEY_BLOCK + order])[None, :]
    j = KEY_BLOCK + a - c
    valid = (j >= 0) & (j <= KEY_BLOCK)
    bias = -slopes[:, None, None] * (dilation * j).astype(F32) * LOG2E
    bias = jnp.where(valid[None], bias, NEG_INF)
    no_prev = jnp.where((c >= KEY_BLOCK)[None], bias, NEG_INF)
    both = jnp.stack([bias, no_prev])
    return both.reshape(2, HEAD_PAIRS, HEADS_PER_TILE * KEY_BLOCK, 2 * KEY_BLOCK)


def _head_expand_matrix():
    src = jnp.arange(LANES)[:, None]
    dst = jnp.arange(GROUP_WIDTH)[None, :]
    e = (src == (dst // HEAD_DIM) * STAT_LANES_PER_HEAD).astype(BF16)
    return jnp.concatenate([e, e], axis=0)


def _piece_permutation():
    dst = jnp.arange(PERM_ROWS)
    src = MAX_DIL * (dst % BF16_ROWS) + dst // BF16_ROWS
    return (jnp.arange(PERM_ROWS)[None, :] == src[:, None]).astype(BF16)


def kernel(x, norm_mix, norm_mlp, w_a_in, conv_w, w_a_out, norm_kv, w_kv, w_q, w_o, w_up, w_down,
           norm_final):
    batch, seq, d = x.shape
    depth = norm_mix.shape[0]
    n_a = w_a_in.shape[0]
    assert seq % SUPER == 0 and d % LANES == 0
    assert [w // dil for w, dil in PATTERNS] == [KEY_BLOCK] * N_GROUPS

    bf = lambda w: w.astype(BF16)
    gain = lambda g: g.reshape(1, d).astype(F32)
    h = x.reshape(batch * seq, d)
    expand = _head_expand_matrix()
    perm = _piece_permutation()
    unperm = perm.T
    idx = jnp.arange(KEY_BLOCK)
    orders = [idx, MID_DIL * (idx % MID_CHUNK) + idx // MID_CHUNK, idx]
    tables = [_alibi_table(dil, order) for (_, dil), order in zip(PATTERNS, orders)]
    q_scale = HEAD_DIM ** -0.5 * LOG2E
    kv_nat = kv_rm = None
    for l in range(depth):
        g_final = gain(norm_final) if l == depth - 1 else None
        if l < n_a:
            h = _conv_mixer(h, gain(norm_mix[l]), bf(w_a_in[l]), conv_w[l].astype(F32),
                            bf(w_a_out[l]), seq=seq)
            h = _mlp(h, gain(norm_mlp[l]), bf(w_up[l]), bf(w_down[l]), g_final, seq=seq)
        else:
            i = l - n_a
            projs = [(gain(norm_mix[l]), bf(w_q[i]), GROUP_WIDTH, q_scale)]
            if kv_nat is None:
                projs.append((gain(norm_kv), bf(w_kv), 2 * GROUP_WIDTH, 1.0))
                q_nat, q_rm, kv_nat, kv_rm = _proj(h, perm, projs, batch=batch, seq=seq)
            else:
                q_nat, q_rm = _proj(h, perm, projs, batch=batch, seq=seq)
            q_nat3 = q_nat.reshape(batch, seq, -1)
            kv_nat3 = kv_nat.reshape(batch, seq, -1)
            accs, stats = [], []
            for g in range(N_GROUPS):
                if g == 0:
                    acc_g, st_g = _attention_group(q_nat3, kv_nat3, tables[g], group=g, q_col=0,
                                                   k_col=0, batch=batch, seq=seq)
                else:
                    acc_g, st_g = _attention_group(q_rm, kv_rm, tables[g], group=g, q_col=g - 1,
                                                   k_col=2 * (g - 1), batch=batch, seq=seq)
                accs.append(acc_g)
                stats.append(st_g)
            h = _mlp(h, gain(norm_mlp[l]), bf(w_up[l]), bf(w_down[l]), g_final, seq=seq,
                     attn=(accs, stats, unperm, expand, bf(w_o[i])))
    return h.reshape(batch, seq, d)
```

```python
import functools

import jax
import jax.numpy as jnp
from jax import lax
from jax.experimental import pallas as pl
from jax.experimental.pallas import tpu as pltpu

PATTERNS = ((128, 1), (512, 4), (2048, 16))
N_GROUPS = len(PATTERNS)
H_G = 8
HEAD_DIM = 64
GROUP_WIDTH = H_G * HEAD_DIM
EPS = 1e-5
ALIBI_MAX_BIAS = 8.0
NEG_INF = -1e30
CONV_WIDTH = 3
LOG2E = 1.4426950408889634

LANES = 128
SUBLANES = 8
BF16_ROWS = 16
KEY_BLOCK = 128
HEADS_PER_TILE = LANES // HEAD_DIM
HEAD_PAIRS = H_G // HEADS_PER_TILE
STAT_LANES_PER_HEAD = LANES // H_G
STAT_HALF = STAT_LANES_PER_HEAD // 2

MAX_DIL = PATTERNS[2][1]
SUPER = PATTERNS[2][0]
MID_DIL = PATTERNS[1][1]
MID_CHUNK = KEY_BLOCK // MID_DIL
TM = 512
TM_WIDE = 1024
PERM_ROWS = MAX_DIL * BF16_ROWS
VMEM_LIMIT_BYTES = 56 * 1024 * 1024

F32 = jnp.float32
BF16 = jnp.bfloat16


def _params(n_axes):
    return pltpu.CompilerParams(
        dimension_semantics=("arbitrary",) * n_axes,
        vmem_limit_bytes=VMEM_LIMIT_BYTES,
    )


def _resident(shape):
    zeros = (0,) * len(shape)
    return pl.BlockSpec(shape, lambda *_: zeros, pipeline_mode=pl.Buffered(1))


def _resident_layer(stacked, layer):
    zeros = (0,) * (stacked.ndim - 1)
    return pl.BlockSpec((None,) + stacked.shape[1:], lambda *_: (layer,) + zeros,
                        pipeline_mode=pl.Buffered(1))


def _rmsnorm(x, g):
    ms = jnp.mean(x * x, axis=-1, keepdims=True)
    return x * lax.rsqrt(ms + EPS) * g


def _split3(x):
    hi = x.astype(BF16)
    r = x - hi.astype(F32)
    mid = r.astype(BF16)
    lo = (r - mid.astype(F32)).astype(BF16)
    return hi, mid, lo


def _to_residue_major(perm, x):
    parts = [jnp.dot(perm, x[s * PERM_ROWS:(s + 1) * PERM_ROWS], preferred_element_type=F32)
             .reshape(MAX_DIL, BF16_ROWS, x.shape[-1]) for s in range(x.shape[0] // PERM_ROWS)]
    return jnp.concatenate(parts, axis=1)


def _to_natural(unperm, x):
    parts = [jnp.dot(unperm, x[:, s * BF16_ROWS:(s + 1) * BF16_ROWS, :].reshape(PERM_ROWS, x.shape[-1]),
                     preferred_element_type=F32) for s in range(x.shape[1] // BF16_ROWS)]
    return jnp.concatenate(parts, axis=0)


def _conv_mixer_kernel(h_ref, g_ref, win_ref, cw_ref, wout_ref, o_ref, cu_ref, *, tiles_per_seq):
    tm, d = h_ref.shape
    x = h_ref[...]
    hn = _rmsnorm(x, g_ref[...]).astype(BF16)
    c_u = jnp.dot(hn, win_ref[:, d:], preferred_element_type=F32)
    cu = c_u[:, :d] * c_u[:, d:]

    @pl.when(pl.program_id(0) % tiles_per_seq == 0)
    def _():
        cu_ref[0:SUBLANES, :] = jnp.zeros((SUBLANES, d), F32)

    cu_ref[SUBLANES:SUBLANES + tm, :] = cu
    cw = cw_ref[...]
    conv = cw[0:1, :] * cu
    for k in range(1, CONV_WIDTH):
        conv = conv + cw[k:k + 1, :] * cu_ref[SUBLANES - k:SUBLANES - k + tm, :]
    b = jnp.dot(hn, win_ref[:, :d], preferred_element_type=F32)
    y = jnp.dot((b * conv).astype(BF16), wout_ref[...], preferred_element_type=F32)
    o_ref[...] = x + y
    cu_ref[0:SUBLANES, :] = cu_ref[tm:tm + SUBLANES, :]


def _conv_mixer(h, g, w_in, conv_w, w_out, layer, *, seq):
    t, d = h.shape
    return pl.pallas_call(
        functools.partial(_conv_mixer_kernel, tiles_per_seq=seq // TM_WIDE),
        grid=(t // TM_WIDE,),
        in_specs=[
            pl.BlockSpec((TM_WIDE, d), lambda i: (i, 0)),
            _resident((1, d)),
            _resident_layer(w_in, layer),
            _resident_layer(conv_w, layer),
            _resident_layer(w_out, layer),
        ],
        out_specs=pl.BlockSpec((TM_WIDE, d), lambda i: (i, 0)),
        out_shape=jax.ShapeDtypeStruct((t, d), F32),
        scratch_shapes=[pltpu.VMEM((TM_WIDE + SUBLANES, d), F32)],
        compiler_params=_params(1),
        name="conv_mixer",
    )(h, g, w_in, conv_w, w_out)


def _mlp_body(x, g_ref, wup_ref, wdn_ref):
    hn = _rmsnorm(x, g_ref[...]).astype(BF16)
    a = jnp.dot(hn, wup_ref[...], preferred_element_type=F32)
    a = jnp.maximum(a, 0.0)
    a = (a * a).astype(BF16)
    return x + jnp.dot(a, wdn_ref[...], preferred_element_type=F32)


def _mlp_kernel(h_ref, g_ref, wup_ref, wdn_ref, *rest, final_norm):
    if final_norm:
        gf_ref, o_ref = rest
    else:
        (o_ref,) = rest
    y = _mlp_body(h_ref[...], g_ref, wup_ref, wdn_ref)
    if final_norm:
        y = _rmsnorm(y, gf_ref[...])
    o_ref[...] = y


def _attn_mlp_kernel(h_ref, o0_ref, o1_ref, o2_ref, s0_ref, s1_ref, s2_ref, unperm_ref, expand_ref,
                     wo_ref, g_ref, wup_ref, wdn_ref, *rest, final_norm):
    if final_norm:
        gf_ref, o_ref = rest
    else:
        (o_ref,) = rest
    tm = h_ref.shape[0]
    unperm = unperm_ref[...]
    accs = [o0_ref[...]]
    stats = [s0_ref[...]]
    for og_ref, sg_ref in ((o1_ref, s1_ref), (o2_ref, s2_ref)):
        accs.append(_to_natural(unperm, og_ref[...]).astype(BF16))
        parts = jnp.concatenate(_split3(sg_ref[...]), axis=2)
        moved = _to_natural(unperm, parts)
        stats.append(moved[:, :LANES] + moved[:, LANES:2 * LANES] + moved[:, 2 * LANES:])
    lane = lax.broadcasted_iota(jnp.int32, (tm, LANES), 1)
    on_max_lanes = (lane % STAT_LANES_PER_HEAD) < STAT_HALF
    m_all = jnp.maximum(jnp.maximum(stats[0], stats[1]), stats[2])
    scale = [jnp.exp2(st - m_all) for st in stats]
    den = None
    for a, st in zip(scale, stats):
        term = a * pltpu.roll(st, LANES - STAT_HALF, axis=1)
        den = term if den is None else den + term
    comb = None
    for a, acc in zip(scale, accs):
        w = jnp.where(on_max_lanes, a / den, 0.0)
        w_hi = w.astype(BF16)
        w_lo = (w - w_hi.astype(F32)).astype(BF16)
        w_full = jnp.dot(jnp.concatenate([w_hi, w_lo], axis=1), expand_ref[...],
                         preferred_element_type=F32)
        term = w_full * acc.astype(F32)
        comb = term if comb is None else comb + term
    x = h_ref[...] + jnp.dot(comb.astype(BF16), wo_ref[...], preferred_element_type=F32)
    y = _mlp_body(x, g_ref, wup_ref, wdn_ref)
    if final_norm:
        y = _rmsnorm(y, gf_ref[...])
    o_ref[...] = y


def _residue_major_spec(width, tm, seq):
    tiles_per_seq, tiles_per_super = seq // tm, SUPER // tm

    def index(t):
        in_seq = t % tiles_per_seq
        return (t // tiles_per_seq, in_seq // tiles_per_super, 0, in_seq % tiles_per_super, 0)
    return pl.BlockSpec((None, None, MAX_DIL, tm // MAX_DIL, width), index)


def _mlp(h, g, w_up, w_down, layer, g_final, *, seq, attn=None):
    t, d = h.shape
    row = lambda width: pl.BlockSpec((TM, width), lambda i: (i, 0))
    args = [h]
    in_specs = [row(d)]
    if attn is not None:
        accs, stats, unperm, expand, w_o, attn_layer = attn
        args += [*accs, *stats, unperm, expand, w_o]
        rm = functools.partial(_residue_major_spec, tm=TM, seq=seq)
        in_specs += [row(GROUP_WIDTH), rm(GROUP_WIDTH), rm(GROUP_WIDTH)]
        in_specs += [row(LANES), rm(LANES), rm(LANES)]
        in_specs += [_resident(unperm.shape), _resident(expand.shape),
                     _resident_layer(w_o, attn_layer)]
        body = _attn_mlp_kernel
    else:
        body = _mlp_kernel
    args += [g, w_up, w_down]
    in_specs += [_resident((1, d)), _resident_layer(w_up, layer), _resident_layer(w_down, layer)]
    if g_final is not None:
        args.append(g_final)
        in_specs.append(_resident((1, d)))
    return pl.pallas_call(
        functools.partial(body, final_norm=g_final is not None),
        grid=(t // TM,),
        in_specs=in_specs,
        out_specs=row(d),
        out_shape=jax.ShapeDtypeStruct((t, d), F32),
        compiler_params=_params(1),
        name="attn_mlp" if attn is not None else "mlp",
    )(*args)


def _proj_kernel(*refs, n_proj, scales):
    h_ref, perm_ref = refs[0], refs[1]
    x = h_ref[...]
    xn = x * lax.rsqrt(jnp.mean(x * x, axis=-1, keepdims=True) + EPS)
    for p in range(n_proj):
        g_ref, w_ref = refs[2 + 2 * p: 4 + 2 * p]
        onat_ref, operm_ref = refs[2 + 2 * n_proj + 2 * p: 4 + 2 * n_proj + 2 * p]
        n_nat = onat_ref.shape[-1]
        hn = (xn * g_ref[...]).astype(BF16)
        y = jnp.dot(hn, w_ref[:, :n_nat], preferred_element_type=F32)
        onat_ref[...] = (y * scales[p]).astype(onat_ref.dtype)
        hn_p = _to_residue_major(perm_ref[...], hn).astype(BF16).reshape(hn.shape)
        y = jnp.dot(hn_p, w_ref[:, n_nat:], preferred_element_type=F32)
        operm_ref[...] = (y * scales[p]).astype(operm_ref.dtype).reshape(operm_ref.shape)


def _proj(h, perm, projs, *, batch, seq):
    t, d = h.shape
    args = [h, perm]
    in_specs = [pl.BlockSpec((TM_WIDE, d), lambda i: (i, 0)), _resident(perm.shape)]
    out_specs, out_shapes = [], []
    for g, w, layer, n_nat, _ in projs:
        args += [g, w]
        in_specs += [_resident((1, d)), _resident_layer(w, layer)]
        n_perm = w.shape[-1] - n_nat
        out_specs.append(pl.BlockSpec((TM_WIDE, n_nat), lambda i: (i, 0)))
        out_shapes.append(jax.ShapeDtypeStruct((t, n_nat), BF16))
        out_specs.append(_residue_major_spec(n_perm, TM_WIDE, seq))
        out_shapes.append(jax.ShapeDtypeStruct(
            (batch, seq // SUPER, MAX_DIL, KEY_BLOCK, n_perm), BF16))
    return pl.pallas_call(
        functools.partial(_proj_kernel, n_proj=len(projs), scales=tuple(p[4] for p in projs)),
        grid=(t // TM_WIDE,),
        in_specs=in_specs,
        out_specs=out_specs,
        out_shape=out_shapes,
        compiler_params=_params(1),
        name="norm_proj",
    )(*args)


def _attn_kernel(q_ref, k_ref, v_ref, bias_ref, o_ref, st_ref, *, layout, chains, chain_len):
    lane = lax.broadcasted_iota(jnp.int32, (KEY_BLOCK, LANES), 1)
    nt = (((1,), (1,)), ((), ()))
    blocks_per_super = SUPER // PATTERNS[1][0]
    ones = jnp.ones((2 * KEY_BLOCK, LANES), BF16)

    def index(c, j, cols):
        if layout == "rows":
            return (slice(j * KEY_BLOCK, (j + 1) * KEY_BLOCK), cols)
        if layout == "chunks":
            return (j // blocks_per_super, slice(None), c, j % blocks_per_super, slice(None), cols)
        return (j, c, slice(None), cols)

    def block(ref, c, j, cols):
        return ref[index(c, j, cols)].reshape(KEY_BLOCK, LANES)

    def store(ref, c, j, cols, val):
        idx = index(c, j, cols)
        if layout == "chunks":
            val = val.reshape(MID_DIL, MID_CHUNK, val.shape[-1])
        ref[idx] = val

    for c in range(chains):
        for j in range(chain_len):
            st_tile = jnp.zeros((KEY_BLOCK, LANES), F32)
            for hp in range(HEAD_PAIRS):
                cols = slice(hp * LANES, (hp + 1) * LANES)
                q2 = block(q_ref, c, j, cols)
                k_cur, v_cur = block(k_ref, c, j, cols), block(v_ref, c, j, cols)
                if j == 0:
                    k_prev, v_prev = k_cur, v_cur
                else:
                    k_prev, v_prev = block(k_ref, c, j - 1, cols), block(v_ref, c, j - 1, cols)
                k2 = jnp.concatenate([k_prev, k_cur], axis=0)
                v2 = jnp.concatenate([jnp.concatenate([v_prev, v_cur], axis=0), ones], axis=1)
                zero = jnp.zeros_like(q2)
                qs = jnp.concatenate([jnp.where(lane < HEAD_DIM, q2, zero),
                                      jnp.where(lane >= HEAD_DIM, q2, zero)], axis=0)
                s = lax.dot_general(qs, k2, nt, preferred_element_type=F32)
                s = s + bias_ref[1 if j == 0 else 0, hp]
                m = jnp.max(s, axis=1, keepdims=True)
                p = jnp.exp2(s - m).astype(BF16)
                pv = jnp.dot(p, v2, preferred_element_type=F32)
                acc, l = pv[:, :LANES], pv[:, LANES:]
                store(o_ref, c, j, cols,
                      jnp.where(lane < HEAD_DIM, acc[:KEY_BLOCK], acc[KEY_BLOCK:]).astype(o_ref.dtype))
                for e in range(HEADS_PER_TILE):
                    h = hp * HEADS_PER_TILE + e
                    rows = slice(e * KEY_BLOCK, (e + 1) * KEY_BLOCK)
                    ml = jnp.where((lane % STAT_LANES_PER_HEAD) < STAT_HALF, m[rows], l[rows])
                    st_tile = jnp.where((lane // STAT_LANES_PER_HEAD) == h, ml, st_tile)
            store(st_ref, c, j, slice(None), st_tile)


def _attention_group(q, kv, bias, *, group, q_col, k_col, batch, seq):
    n_super = seq // SUPER
    dil = PATTERNS[group][1]
    chains, chain_len = dil, seq // (dil * KEY_BLOCK)
    if group == 0:
        layout = "rows"
        shape = lambda w: (batch, seq, w)
    elif group == 1:
        layout = "chunks"
        shape = lambda w: (batch, n_super, MID_DIL, MID_DIL, SUPER // PATTERNS[1][0], MID_CHUNK, w)
    else:
        layout = "blocks"
        shape = lambda w: (batch, n_super, MAX_DIL, KEY_BLOCK, w)
    n_lead = len(shape(1)) - 2

    def spec(col, w=GROUP_WIDTH):
        return pl.BlockSpec((None,) + shape(w)[1:-1] + (w,), lambda b: (b,) + (0,) * n_lead + (col,))

    q, kv = q.reshape(shape(q.shape[-1])), kv.reshape(shape(kv.shape[-1]))
    acc, stats = pl.pallas_call(
        functools.partial(_attn_kernel, layout=layout, chains=chains, chain_len=chain_len),
        grid=(batch,),
        in_specs=[spec(q_col), spec(k_col), spec(k_col + 1), _resident(bias.shape)],
        out_specs=[spec(0), spec(0, LANES)],
        out_shape=[jax.ShapeDtypeStruct(shape(GROUP_WIDTH), BF16),
                   jax.ShapeDtypeStruct(shape(LANES), F32)],
        compiler_params=_params(1),
        name=f"dilated_attn_g{group}",
    )(q, kv, kv, bias)
    if group == 0:
        return acc.reshape(batch * seq, GROUP_WIDTH), stats.reshape(batch * seq, LANES)
    rm = lambda a: a.reshape(batch, n_super, MAX_DIL, KEY_BLOCK, a.shape[-1])
    return rm(acc), rm(stats)


def _alibi_table(dilation, order):
    slopes = 2.0 ** (-ALIBI_MAX_BIAS * jnp.arange(1, H_G + 1, dtype=F32) / H_G)
    a = order[:, None]
    c = jnp.concatenate([order, KEY_BLOCK + order])[None, :]
    j = KEY_BLOCK + a - c
    valid = (j >= 0) & (j <= KEY_BLOCK)
    bias = -slopes[:, None, None] * (dilation * j).astype(F32) * LOG2E
    bias = jnp.where(valid[None], bias, NEG_INF)
    no_prev = jnp.where((c >= KEY_BLOCK)[None], bias, NEG_INF)
    both = jnp.stack([bias, no_prev])
    return both.reshape(2, HEAD_PAIRS, HEADS_PER_TILE * KEY_BLOCK, 2 * KEY_BLOCK)


def _head_expand_matrix():
    src = jnp.arange(LANES)[:, None]
    dst = jnp.arange(GROUP_WIDTH)[None, :]
    e = (src == (dst // HEAD_DIM) * STAT_LANES_PER_HEAD).astype(BF16)
    return jnp.concatenate([e, e], axis=0)


def _piece_permutation():
    dst = jnp.arange(PERM_ROWS)
    src = MAX_DIL * (dst % BF16_ROWS) + dst // BF16_ROWS
    return (jnp.arange(PERM_ROWS)[None, :] == src[:, None]).astype(BF16)


def kernel(x, norm_mix, norm_mlp, w_a_in, conv_w, w_a_out, norm_kv, w_kv, w_q, w_o, w_up, w_down,
           norm_final):
    batch, seq, d = x.shape
    depth = norm_mix.shape[0]
    n_a = w_a_in.shape[0]
    assert seq % SUPER == 0 and d % LANES == 0
    assert [w // dil for w, dil in PATTERNS] == [KEY_BLOCK] * N_GROUPS

    bf = lambda w: w.astype(BF16)
    gain = lambda g: g.reshape(1, d).astype(F32)
    h = x.reshape(batch * seq, d)
    expand = _head_expand_matrix()
    perm = _piece_permutation()
    unperm = perm.T
    idx = jnp.arange(KEY_BLOCK)
    orders = [idx, MID_DIL * (idx % MID_CHUNK) + idx // MID_CHUNK, idx]
    tables = [_alibi_table(dil, order) for (_, dil), order in zip(PATTERNS, orders)]
    q_scale = HEAD_DIM ** -0.5 * LOG2E
    w_a_in, w_a_out, w_q, w_o, w_up, w_down = map(bf, (w_a_in, w_a_out, w_q, w_o, w_up, w_down))
    w_kv = bf(w_kv)[None]
    conv_w = conv_w.astype(F32)
    kv_nat = kv_rm = None
    for l in range(depth):
        g_final = gain(norm_final) if l == depth - 1 else None
        if l < n_a:
            h = _conv_mixer(h, gain(norm_mix[l]), w_a_in, conv_w, w_a_out, l, seq=seq)
            h = _mlp(h, gain(norm_mlp[l]), w_up, w_down, l, g_final, seq=seq)
        else:
            i = l - n_a
            projs = [(gain(norm_mix[l]), w_q, i, GROUP_WIDTH, q_scale)]
            if kv_nat is None:
                projs.append((gain(norm_kv), w_kv, 0, 2 * GROUP_WIDTH, 1.0))
                q_nat, q_rm, kv_nat, kv_rm = _proj(h, perm, projs, batch=batch, seq=seq)
            else:
                q_nat, q_rm = _proj(h, perm, projs, batch=batch, seq=seq)
            q_nat3 = q_nat.reshape(batch, seq, -1)
            kv_nat3 = kv_nat.reshape(batch, seq, -1)
            accs, stats = [], []
            for g in range(N_GROUPS):
                if g == 0:
                    acc_g, st_g = _attention_group(q_nat3, kv_nat3, tables[g], group=g, q_col=0,
                                                   k_col=0, batch=batch, seq=seq)
                else:
                    acc_g, st_g = _attention_group(q_rm, kv_rm, tables[g], group=g, q_col=g - 1,
                                                   k_col=2 * (g - 1), batch=batch, seq=seq)
                accs.append(acc_g)
                stats.append(st_g)
            h = _mlp(h, gain(norm_mlp[l]), w_up, w_down, l, g_final, seq=seq,
                     attn=(accs, stats, unperm, expand, w_o, i))
    return h.reshape(batch, seq, d)
```

```python
import functools

import jax
import jax.numpy as jnp
from jax import lax
from jax.experimental import pallas as pl
from jax.experimental.pallas import tpu as pltpu

PATTERNS = ((128, 1), (512, 4), (2048, 16))
N_GROUPS = len(PATTERNS)
H_G = 8
HEAD_DIM = 64
GROUP_WIDTH = H_G * HEAD_DIM
EPS = 1e-5
ALIBI_MAX_BIAS = 8.0
NEG_INF = -1e30
CONV_WIDTH = 3
LOG2E = 1.4426950408889634

LANES = 128
SUBLANES = 8
BF16_ROWS = 16
KEY_BLOCK = 128
HEADS_PER_TILE = LANES // HEAD_DIM
HEAD_PAIRS = H_G // HEADS_PER_TILE
STAT_LANES_PER_HEAD = LANES // H_G
STAT_HALF = STAT_LANES_PER_HEAD // 2

MAX_DIL = PATTERNS[2][1]
SUPER = PATTERNS[2][0]
MID_DIL = PATTERNS[1][1]
MID_CHUNK = KEY_BLOCK // MID_DIL
TM = 512
TM_WIDE = 1024
PERM_ROWS = MAX_DIL * BF16_ROWS
VMEM_LIMIT_BYTES = 56 * 1024 * 1024

F32 = jnp.float32
BF16 = jnp.bfloat16


def _params(n_axes):
    return pltpu.CompilerParams(
        dimension_semantics=("arbitrary",) * n_axes,
        vmem_limit_bytes=VMEM_LIMIT_BYTES,
    )


def _resident(shape):
    zeros = (0,) * len(shape)
    return pl.BlockSpec(shape, lambda *_: zeros, pipeline_mode=pl.Buffered(1))


def _resident_layer(stacked, layer):
    zeros = (0,) * (stacked.ndim - 1)
    return pl.BlockSpec((None,) + stacked.shape[1:], lambda *_: (layer,) + zeros,
                        pipeline_mode=pl.Buffered(1))


def _rmsnorm(x, g):
    ms = jnp.mean(x * x, axis=-1, keepdims=True)
    return x * lax.rsqrt(ms + EPS) * g


def _split3(x):
    hi = x.astype(BF16)
    r = x - hi.astype(F32)
    mid = r.astype(BF16)
    lo = (r - mid.astype(F32)).astype(BF16)
    return hi, mid, lo


def _to_residue_major(perm, x):
    parts = [jnp.dot(perm, x[s * PERM_ROWS:(s + 1) * PERM_ROWS], preferred_element_type=F32)
             .reshape(MAX_DIL, BF16_ROWS, x.shape[-1]) for s in range(x.shape[0] // PERM_ROWS)]
    return jnp.concatenate(parts, axis=1)


def _to_natural(unperm, x):
    parts = [jnp.dot(unperm, x[:, s * BF16_ROWS:(s + 1) * BF16_ROWS, :].reshape(PERM_ROWS, x.shape[-1]),
                     preferred_element_type=F32) for s in range(x.shape[1] // BF16_ROWS)]
    return jnp.concatenate(parts, axis=0)


def _conv_mixer_kernel(h_ref, g_ref, win_ref, cw_ref, wout_ref, o_ref, cu_ref, *, tiles_per_seq):
    tm, d = h_ref.shape
    x = h_ref[...]
    hn = _rmsnorm(x, g_ref[...]).astype(BF16)
    c_u = jnp.dot(hn, win_ref[:, d:], preferred_element_type=F32)
    cu = c_u[:, :d] * c_u[:, d:]

    @pl.when(pl.program_id(0) % tiles_per_seq == 0)
    def _():
        cu_ref[0:SUBLANES, :] = jnp.zeros((SUBLANES, d), F32)

    cu_ref[SUBLANES:SUBLANES + tm, :] = cu
    cw = cw_ref[...]
    conv = cw[0:1, :] * cu
    for k in range(1, CONV_WIDTH):
        conv = conv + cw[k:k + 1, :] * cu_ref[SUBLANES - k:SUBLANES - k + tm, :]
    b = jnp.dot(hn, win_ref[:, :d], preferred_element_type=F32)
    y = jnp.dot((b * conv).astype(BF16), wout_ref[...], preferred_element_type=F32)
    o_ref[...] = x + y
    cu_ref[0:SUBLANES, :] = cu_ref[tm:tm + SUBLANES, :]


def _conv_mixer(h, g, w_in, conv_w, w_out, layer, *, seq):
    t, d = h.shape
    return pl.pallas_call(
        functools.partial(_conv_mixer_kernel, tiles_per_seq=seq // TM_WIDE),
        grid=(t // TM_WIDE,),
        in_specs=[
            pl.BlockSpec((TM_WIDE, d), lambda i: (i, 0)),
            _resident((1, d)),
            _resident_layer(w_in, layer),
            _resident_layer(conv_w, layer),
            _resident_layer(w_out, layer),
        ],
        out_specs=pl.BlockSpec((TM_WIDE, d), lambda i: (i, 0)),
        out_shape=jax.ShapeDtypeStruct((t, d), F32),
        scratch_shapes=[pltpu.VMEM((TM_WIDE + SUBLANES, d), F32)],
        compiler_params=_params(1),
        name="conv_mixer",
    )(h, g, w_in, conv_w, w_out)


def _mlp_body(x, g_ref, wup_ref, wdn_ref):
    hn = _rmsnorm(x, g_ref[...]).astype(BF16)
    a = jnp.dot(hn, wup_ref[...], preferred_element_type=F32)
    a = jnp.maximum(a, 0.0)
    a = (a * a).astype(BF16)
    return x + jnp.dot(a, wdn_ref[...], preferred_element_type=F32)


def _mlp_kernel(h_ref, g_ref, wup_ref, wdn_ref, *rest, final_norm):
    if final_norm:
        gf_ref, o_ref = rest
    else:
        (o_ref,) = rest
    for r in range(h_ref.shape[0] // TM):
        rows = slice(r * TM, (r + 1) * TM)
        y = _mlp_body(h_ref[rows, :], g_ref, wup_ref, wdn_ref)
        if final_norm:
            y = _rmsnorm(y, gf_ref[...])
        o_ref[rows, :] = y


def _attn_mlp_kernel(h_ref, o0_ref, o1_ref, o2_ref, s0_ref, s1_ref, s2_ref, unperm_ref, expand_ref,
                     wo_ref, g_ref, wup_ref, wdn_ref, *rest, final_norm):
    if final_norm:
        gf_ref, o_ref = rest
    else:
        (o_ref,) = rest
    tm = h_ref.shape[0]
    unperm = unperm_ref[...]
    accs = [o0_ref[...]]
    stats = [s0_ref[...]]
    for og_ref, sg_ref in ((o1_ref, s1_ref), (o2_ref, s2_ref)):
        accs.append(_to_natural(unperm, og_ref[...]).astype(BF16))
        parts = jnp.concatenate(_split3(sg_ref[...]), axis=2)
        moved = _to_natural(unperm, parts)
        stats.append(moved[:, :LANES] + moved[:, LANES:2 * LANES] + moved[:, 2 * LANES:])
    lane = lax.broadcasted_iota(jnp.int32, (tm, LANES), 1)
    on_max_lanes = (lane % STAT_LANES_PER_HEAD) < STAT_HALF
    m_all = jnp.maximum(jnp.maximum(stats[0], stats[1]), stats[2])
    scale = [jnp.exp2(st - m_all) for st in stats]
    den = None
    for a, st in zip(scale, stats):
        term = a * pltpu.roll(st, LANES - STAT_HALF, axis=1)
        den = term if den is None else den + term
    comb = None
    for a, acc in zip(scale, accs):
        w = jnp.where(on_max_lanes, a / den, 0.0)
        w_hi = w.astype(BF16)
        w_lo = (w - w_hi.astype(F32)).astype(BF16)
        w_full = jnp.dot(jnp.concatenate([w_hi, w_lo], axis=1), expand_ref[...],
                         preferred_element_type=F32)
        term = w_full * acc.astype(F32)
        comb = term if comb is None else comb + term
    x = h_ref[...] + jnp.dot(comb.astype(BF16), wo_ref[...], preferred_element_type=F32)
    y = _mlp_body(x, g_ref, wup_ref, wdn_ref)
    if final_norm:
        y = _rmsnorm(y, gf_ref[...])
    o_ref[...] = y


def _residue_major_spec(width, tm, seq):
    tiles_per_seq, tiles_per_super = seq // tm, SUPER // tm

    def index(t):
        in_seq = t % tiles_per_seq
        return (t // tiles_per_seq, in_seq // tiles_per_super, 0, in_seq % tiles_per_super, 0)
    return pl.BlockSpec((None, None, MAX_DIL, tm // MAX_DIL, width), index)


def _mlp(h, g, w_up, w_down, layer, g_final, *, seq, attn=None):
    t, d = h.shape
    tm = TM if attn is not None else TM_WIDE
    row = lambda width: pl.BlockSpec((tm, width), lambda i: (i, 0))
    args = [h]
    in_specs = [row(d)]
    if attn is not None:
        accs, stats, unperm, expand, w_o, attn_layer = attn
        args += [*accs, *stats, unperm, expand, w_o]
        rm = functools.partial(_residue_major_spec, tm=TM, seq=seq)
        in_specs += [row(GROUP_WIDTH), rm(GROUP_WIDTH), rm(GROUP_WIDTH)]
        in_specs += [row(LANES), rm(LANES), rm(LANES)]
        in_specs += [_resident(unperm.shape), _resident(expand.shape),
                     _resident_layer(w_o, attn_layer)]
        body = _attn_mlp_kernel
    else:
        body = _mlp_kernel
    args += [g, w_up, w_down]
    in_specs += [_resident((1, d)), _resident_layer(w_up, layer), _resident_layer(w_down, layer)]
    if g_final is not None:
        args.append(g_final)
        in_specs.append(_resident((1, d)))
    return pl.pallas_call(
        functools.partial(body, final_norm=g_final is not None),
        grid=(t // tm,),
        in_specs=in_specs,
        out_specs=row(d),
        out_shape=jax.ShapeDtypeStruct((t, d), F32),
        compiler_params=_params(1),
        name="attn_mlp" if attn is not None else "mlp",
    )(*args)


def _proj_kernel(*refs, n_proj, scales):
    h_ref, perm_ref = refs[0], refs[1]
    x = h_ref[...]
    xn = x * lax.rsqrt(jnp.mean(x * x, axis=-1, keepdims=True) + EPS)
    for p in range(n_proj):
        g_ref, w_ref = refs[2 + 2 * p: 4 + 2 * p]
        onat_ref, operm_ref = refs[2 + 2 * n_proj + 2 * p: 4 + 2 * n_proj + 2 * p]
        n_nat = onat_ref.shape[-1]
        hn = (xn * g_ref[...]).astype(BF16)
        y = jnp.dot(hn, w_ref[:, :n_nat], preferred_element_type=F32)
        onat_ref[...] = (y * scales[p]).astype(onat_ref.dtype)
        hn_p = _to_residue_major(perm_ref[...], hn).astype(BF16).reshape(hn.shape)
        y = jnp.dot(hn_p, w_ref[:, n_nat:], preferred_element_type=F32)
        operm_ref[...] = (y * scales[p]).astype(operm_ref.dtype).reshape(operm_ref.shape)


def _proj(h, perm, projs, *, batch, seq):
    t, d = h.shape
    args = [h, perm]
    in_specs = [pl.BlockSpec((TM_WIDE, d), lambda i: (i, 0)), _resident(perm.shape)]
    out_specs, out_shapes = [], []
    for g, w, layer, n_nat, _ in projs:
        args += [g, w]
        in_specs += [_resident((1, d)), _resident_layer(w, layer)]
        n_perm = w.shape[-1] - n_nat
        out_specs.append(pl.BlockSpec((TM_WIDE, n_nat), lambda i: (i, 0)))
        out_shapes.append(jax.ShapeDtypeStruct((t, n_nat), BF16))
        out_specs.append(_residue_major_spec(n_perm, TM_WIDE, seq))
        out_shapes.append(jax.ShapeDtypeStruct(
            (batch, seq // SUPER, MAX_DIL, KEY_BLOCK, n_perm), BF16))
    return pl.pallas_call(
        functools.partial(_proj_kernel, n_proj=len(projs), scales=tuple(p[4] for p in projs)),
        grid=(t // TM_WIDE,),
        in_specs=in_specs,
        out_specs=out_specs,
        out_shape=out_shapes,
        compiler_params=_params(1),
        name="norm_proj",
    )(*args)


def _attn_kernel(q_ref, k_ref, v_ref, bias_ref, o_ref, st_ref, *, layout, chains, chain_len):
    lane = lax.broadcasted_iota(jnp.int32, (KEY_BLOCK, LANES), 1)
    nt = (((1,), (1,)), ((), ()))
    blocks_per_super = SUPER // PATTERNS[1][0]
    ones = jnp.ones((2 * KEY_BLOCK, LANES), BF16)

    def index(c, j, cols):
        if layout == "rows":
            return (slice(j * KEY_BLOCK, (j + 1) * KEY_BLOCK), cols)
        if layout == "chunks":
            return (j // blocks_per_super, slice(None), c, j % blocks_per_super, slice(None), cols)
        return (j, c, slice(None), cols)

    def block(ref, c, j, cols):
        return ref[index(c, j, cols)].reshape(KEY_BLOCK, LANES)

    def store(ref, c, j, cols, val):
        idx = index(c, j, cols)
        if layout == "chunks":
            val = val.reshape(MID_DIL, MID_CHUNK, val.shape[-1])
        ref[idx] = val

    for c in range(chains):
        for j in range(chain_len):
            st_tile = jnp.zeros((KEY_BLOCK, LANES), F32)
            for hp in range(HEAD_PAIRS):
                cols = slice(hp * LANES, (hp + 1) * LANES)
                q2 = block(q_ref, c, j, cols)
                k_cur, v_cur = block(k_ref, c, j, cols), block(v_ref, c, j, cols)
                if j == 0:
                    k_prev, v_prev = k_cur, v_cur
                else:
                    k_prev, v_prev = block(k_ref, c, j - 1, cols), block(v_ref, c, j - 1, cols)
                k2 = jnp.concatenate([k_prev, k_cur], axis=0)
                v2 = jnp.concatenate([jnp.concatenate([v_prev, v_cur], axis=0), ones], axis=1)
                zero = jnp.zeros_like(q2)
                qs = jnp.concatenate([jnp.where(lane < HEAD_DIM, q2, zero),
                                      jnp.where(lane >= HEAD_DIM, q2, zero)], axis=0)
                s = lax.dot_general(qs, k2, nt, preferred_element_type=F32)
                s = s + bias_ref[1 if j == 0 else 0, hp]
                m = jnp.max(s, axis=1, keepdims=True)
                p = jnp.exp2(s - m).astype(BF16)
                pv = jnp.dot(p, v2, preferred_element_type=F32)
                acc, l = pv[:, :LANES], pv[:, LANES:]
                store(o_ref, c, j, cols,
                      jnp.where(lane < HEAD_DIM, acc[:KEY_BLOCK], acc[KEY_BLOCK:]).astype(o_ref.dtype))
                for e in range(HEADS_PER_TILE):
                    h = hp * HEADS_PER_TILE + e
                    rows = slice(e * KEY_BLOCK, (e + 1) * KEY_BLOCK)
                    ml = jnp.where((lane % STAT_LANES_PER_HEAD) < STAT_HALF, m[rows], l[rows])
                    st_tile = jnp.where((lane // STAT_LANES_PER_HEAD) == h, ml, st_tile)
            store(st_ref, c, j, slice(None), st_tile)


def _attention_group(q, kv, bias, *, group, q_col, k_col, batch, seq):
    n_super = seq // SUPER
    dil = PATTERNS[group][1]
    chains, chain_len = dil, seq // (dil * KEY_BLOCK)
    if group == 0:
        layout = "rows"
        shape = lambda w: (batch, seq, w)
    elif group == 1:
        layout = "chunks"
        shape = lambda w: (batch, n_super, MID_DIL, MID_DIL, SUPER // PATTERNS[1][0], MID_CHUNK, w)
    else:
        layout = "blocks"
        shape = lambda w: (batch, n_super, MAX_DIL, KEY_BLOCK, w)
    n_lead = len(shape(1)) - 2

    def spec(col, w=GROUP_WIDTH):
        return pl.BlockSpec((None,) + shape(w)[1:-1] + (w,), lambda b: (b,) + (0,) * n_lead + (col,))

    q, kv = q.reshape(shape(q.shape[-1])), kv.reshape(shape(kv.shape[-1]))
    acc, stats = pl.pallas_call(
        functools.partial(_attn_kernel, layout=layout, chains=chains, chain_len=chain_len),
        grid=(batch,),
        in_specs=[spec(q_col), spec(k_col), spec(k_col + 1), _resident(bias.shape)],
        out_specs=[spec(0), spec(0, LANES)],
        out_shape=[jax.ShapeDtypeStruct(shape(GROUP_WIDTH), BF16),
                   jax.ShapeDtypeStruct(shape(LANES), F32)],
        compiler_params=_params(1),
        name=f"dilated_attn_g{group}",
    )(q, kv, kv, bias)
    if group == 0:
        return acc.reshape(batch * seq, GROUP_WIDTH), stats.reshape(batch * seq, LANES)
    rm = lambda a: a.reshape(batch, n_super, MAX_DIL, KEY_BLOCK, a.shape[-1])
    return rm(acc), rm(stats)


def _alibi_table(dilation, order):
    slopes = 2.0 ** (-ALIBI_MAX_BIAS * jnp.arange(1, H_G + 1, dtype=F32) / H_G)
    a = order[:, None]
    c = jnp.concatenate([order, KEY_BLOCK + order])[None, :]
    j = KEY_BLOCK + a - c
    valid = (j >= 0) & (j <= KEY_BLOCK)
    bias = -slopes[:, None, None] * (dilation * j).astype(F32) * LOG2E
    bias = jnp.where(valid[None], bias, NEG_INF)
    no_prev = jnp.where((c >= KEY_BLOCK)[None], bias, NEG_INF)
    both = jnp.stack([bias, no_prev])
    return both.reshape(2, HEAD_PAIRS, HEADS_PER_TILE * KEY_BLOCK, 2 * KEY_BLOCK)


def _head_expand_matrix():
    src = jnp.arange(LANES)[:, None]
    dst = jnp.arange(GROUP_WIDTH)[None, :]
    e = (src == (dst // HEAD_DIM) * STAT_LANES_PER_HEAD).astype(BF16)
    return jnp.concatenate([e, e], axis=0)


def _piece_permutation():
    dst = jnp.arange(PERM_ROWS)
    src = MAX_DIL * (dst % BF16_ROWS) + dst // BF16_ROWS
    return (jnp.arange(PERM_ROWS)[None, :] == src[:, None]).astype(BF16)


def kernel(x, norm_mix, norm_mlp, w_a_in, conv_w, w_a_out, norm_kv, w_kv, w_q, w_o, w_up, w_down,
           norm_final):
    batch, seq, d = x.shape
    depth = norm_mix.shape[0]
    n_a = w_a_in.shape[0]
    assert seq % SUPER == 0 and d % LANES == 0
    assert [w // dil for w, dil in PATTERNS] == [KEY_BLOCK] * N_GROUPS

    bf = lambda w: w.astype(BF16)
    gain = lambda g: g.reshape(1, d).astype(F32)
    h = x.reshape(batch * seq, d)
    expand = _head_expand_matrix()
    perm = _piece_permutation()
    unperm = perm.T
    idx = jnp.arange(KEY_BLOCK)
    orders = [idx, MID_DIL * (idx % MID_CHUNK) + idx // MID_CHUNK, idx]
    tables = [_alibi_table(dil, order) for (_, dil), order in zip(PATTERNS, orders)]
    q_scale = HEAD_DIM ** -0.5 * LOG2E
    w_a_in, w_a_out, w_q, w_o, w_up, w_down = map(bf, (w_a_in, w_a_out, w_q, w_o, w_up, w_down))
    w_kv = bf(w_kv)[None]
    conv_w = conv_w.astype(F32)
    kv_nat = kv_rm = None
    for l in range(depth):
        g_final = gain(norm_final) if l == depth - 1 else None
        if l < n_a:
            h = _conv_mixer(h, gain(norm_mix[l]), w_a_in, conv_w, w_a_out, l, seq=seq)
            h = _mlp(h, gain(norm_mlp[l]), w_up, w_down, l, g_final, seq=seq)
        else:
            i = l - n_a
            projs = [(gain(norm_mix[l]), w_q, i, GROUP_WIDTH, q_scale)]
            if kv_nat is None:
                projs.append((gain(norm_kv), w_kv, 0, 2 * GROUP_WIDTH, 1.0))
                q_nat, q_rm, kv_nat, kv_rm = _proj(h, perm, projs, batch=batch, seq=seq)
            else:
                q_nat, q_rm = _proj(h, perm, projs, batch=batch, seq=seq)
            q_nat3 = q_nat.reshape(batch, seq, -1)
            kv_nat3 = kv_nat.reshape(batch, seq, -1)
            accs, stats = [], []
            for g in range(N_GROUPS):
                if g == 0:
                    acc_g, st_g = _attention_group(q_nat3, kv_nat3, tables[g], group=g, q_col=0,
                                                   k_col=0, batch=batch, seq=seq)
                else:
                    acc_g, st_g = _attention_group(q_rm, kv_rm, tables[g], group=g, q_col=g - 1,
                                                   k_col=2 * (g - 1), batch=batch, seq=seq)
                accs.append(acc_g)
                stats.append(st_g)
            h = _mlp(h, gain(norm_mlp[l]), w_up, w_down, l, g_final, seq=seq,
                     attn=(accs, stats, unperm, expand, w_o, i))
    return h.reshape(batch, seq, d)
```
